```python
import math
import jax
import jax.numpy as jnp
from jax import lax
import numpy as np

D_MODEL = 2048
BATCH = 2
SEQ = 4096
DEPTH = 4
DEC_BATCH = 8
DEC_SEQ = 1
PAST_LEN = 16384
PAGE_SIZE = 128

N_A = DEPTH // 2
N_B = DEPTH - N_A
MIX_W = D_MODEL
MEM_TOKENS = 256
MEM_HEADS = 4
MEM_W = MIX_W // 4
MEM_DH = MEM_W // MEM_HEADS
RWKV_W = MIX_W - MEM_W
RWKV_N = 64
RWKV_HEADS = RWKV_W // RWKV_N
LORA_W = max(32, int(round(D_MODEL ** 0.5 * 1.8 / 32)) * 32)
LORA_A = max(32, int(round(D_MODEL ** 0.5 * 1.8 / 32)) * 32)
LORA_G = max(32, int(round(D_MODEL ** 0.8 * 0.6 / 32)) * 32)
A_IN = 3 * RWKV_W + MEM_W
DIFF_W = MIX_W - MEM_W
DIFF_DV = 128
DIFF_HEADS = DIFF_W // DIFF_DV
DIFF_DH = DIFF_DV // 2
ROT_DIM = DIFF_DH // 4
ROPE_THETA = 500000.0
D_FF = 11 * D_MODEL // 4
Q_BLOCK = 128
NORM_EPS = 1e-6
LNX_EPS = 64e-5
SUBLN_EPS = 1e-5

kernel_name = 'yoco_rwkv7_diffattn_macaron_step'


def rms_norm(x, g, eps=NORM_EPS):
    xf = x.astype(jnp.float32)
    y = xf * lax.rsqrt(jnp.mean(xf * xf, axis=-1, keepdims=True) + eps)
    return (y * g.astype(jnp.float32)).astype(x.dtype)


def swiglu_half(x, g, w13, w2):
    gate, up = jnp.split(rms_norm(x, g) @ w13, 2, axis=-1)
    return x + 0.5 * ((jax.nn.silu(gate) * up) @ w2)


def rope_tables(pos):
    inv = ROPE_THETA ** (-jnp.arange(0, ROT_DIM, 2, dtype=jnp.float32) / ROT_DIM)
    ang = pos.astype(jnp.float32)[:, None] * inv[None, :]
    return jnp.cos(ang), jnp.sin(ang)


def partial_rope(x, cos, sin):
    half = ROT_DIM // 2
    c, s = cos[:, None, None, :], sin[:, None, None, :]
    xf = x.astype(jnp.float32)
    x1, x2, xp = xf[..., :half], xf[..., half:ROT_DIM], xf[..., ROT_DIM:]
    return jnp.concatenate([x1 * c - x2 * s, x2 * c + x1 * s, xp], axis=-1).astype(x.dtype)


def memory_kv(mem, g_mem, w_kv, g_k):
    b, m, _ = mem.shape
    k, v = jnp.split(rms_norm(mem, g_mem) @ w_kv, 2, axis=-1)
    k = rms_norm(k.reshape(b, m, MEM_HEADS, MEM_DH), g_k)
    return k, v.reshape(b, m, MEM_HEADS, MEM_DH)


def memory_attend(q, mk, mv, g_q):
    b, t, _ = q.shape
    qh = rms_norm(q.reshape(b, t, MEM_HEADS, MEM_DH), g_q)
    s = jnp.einsum('bthd,bmhd->bhtm', qh, mk).astype(jnp.float32) * (MEM_DH ** -0.5)
    p = jax.nn.softmax(s, axis=-1)
    o = jnp.einsum('bhtm,bmhd->bthd', p, mv.astype(jnp.float32))
    return o.reshape(b, t, MEM_W).astype(q.dtype)


def wkv_scan(s0, r, decay, k, v, a_vec, b_vec):
    def step(s, inp):
        rt, wt, kt, vt, at, bt = inp
        sa = jnp.einsum('bhij,bhj->bhi', s, at)
        s = s * wt[:, :, None, :] + sa[..., None] * bt[:, :, None, :] + vt[..., None] * kt[:, :, None, :]
        return s, jnp.einsum('bhij,bhj->bhi', s, rt)
    xs = tuple(jnp.moveaxis(t, 1, 0) for t in (r, decay, k, v, a_vec, b_vec))
    s_fin, ys = lax.scan(step, s0, xs)
    return jnp.moveaxis(ys, 0, 1), s_fin


def rwkv7_mix(xn, shift_prev, s0, w_in, mu, w0, w1, w2, a0, a1, a2, g1, g2, k_k, k_a, r_k, lnx_w, lnx_b):
    b, t, _ = xn.shape
    f32 = jnp.float32
    x_prev = jnp.concatenate([shift_prev[:, None, :].astype(xn.dtype), xn[:, :-1]], axis=1)
    xx = x_prev - xn
    proj = jnp.concatenate([xn, xx], axis=-1) @ w_in
    r, k, v, q_mem = jnp.split(proj, [RWKV_W, 2 * RWKV_W, 3 * RWKV_W], axis=-1)
    xw, xa, xg = xn + xx * mu[0], xn + xx * mu[1], xn + xx * mu[2]
    w = -jax.nn.softplus(-(w0 + jnp.tanh(xw @ w1) @ w2)) - 0.5
    decay = jnp.exp(-jnp.exp(w.astype(f32)))
    a = jax.nn.sigmoid(a0 + (xa @ a1) @ a2)
    g = jax.nn.sigmoid(xg @ g1) @ g2
    heads = lambda z: z.reshape(b, t, RWKV_HEADS, RWKV_N).astype(f32)
    kk = heads(k * k_k)
    kk = kk / jnp.maximum(jnp.sqrt(jnp.sum(kk * kk, axis=-1, keepdims=True)), 1e-12)
    k = k * (1.0 + (a - 1.0) * k_a)
    rh, kh, vh, ah = heads(r), heads(k), heads(v), heads(a)
    y, s_fin = wkv_scan(s0.astype(f32), rh, heads(decay), kh, vh, -kk, kk * ah)
    mean = jnp.mean(y, axis=-1, keepdims=True)
    var = jnp.mean(jnp.square(y - mean), axis=-1, keepdims=True)
    y = ((y - mean) * lax.rsqrt(var + LNX_EPS)).reshape(b, t, RWKV_W)
    y = y * lnx_w.astype(f32) + lnx_b.astype(f32)
    y = y + (jnp.sum(rh * kh * r_k.astype(f32), axis=-1, keepdims=True) * vh).reshape(b, t, RWKV_W)
    return (y * g.astype(f32)).astype(xn.dtype), q_mem, s_fin


def shared_kv(h, g, w_kv, g_k, cos, sin):
    b, t, _ = h.shape
    k, v = jnp.split(rms_norm(h, g) @ w_kv, 2, axis=-1)
    k = partial_rope(rms_norm(k.reshape(b, t, DIFF_HEADS, 2, DIFF_DH), g_k), cos, sin)
    return k, v.reshape(b, t, DIFF_HEADS, DIFF_DV)


def diff_attention_prompt(q, k, v, lam):
    b, t = q.shape[:2]
    nb = t // Q_BLOCK
    scale = DIFF_DH ** -0.5
    key_pos = jnp.arange(t)
    qb = jnp.moveaxis(q.reshape(b, nb, Q_BLOCK, DIFF_HEADS, 2, DIFF_DH), 1, 0)
    vf = v.astype(jnp.float32)

    def block(args):
        qi, i = args
        s = jnp.einsum('bqhcd,bkhcd->bhcqk', qi, k).astype(jnp.float32) * scale
        q_pos = i * Q_BLOCK + jnp.arange(Q_BLOCK)
        s = jnp.where(key_pos[None, :] <= q_pos[:, None], s, -jnp.inf)
        p = jax.nn.softmax(s, axis=-1)
        att = p[:, :, 0] - lam * p[:, :, 1]
        return jnp.einsum('bhqk,bkhd->bqhd', att, vf)

    o = lax.map(block, (qb, jnp.arange(nb)))
    return jnp.moveaxis(o, 0, 1).reshape(b, t, DIFF_HEADS, DIFF_DV).astype(q.dtype)


def diff_attention_sample(q, past_k, past_v, k_new, v_new, lam):
    scale = DIFF_DH ** -0.5
    t = q.shape[1]
    p_len = past_k.shape[1]
    s_past = jnp.einsum('bqhcd,bkhcd->bhcqk', q, past_k).astype(jnp.float32) * scale
    s_new = jnp.einsum('bqhcd,bkhcd->bhcqk', q, k_new).astype(jnp.float32) * scale
    s_new = jnp.where(jnp.tril(jnp.ones((t, t), dtype=bool)), s_new, -jnp.inf)
    p = jax.nn.softmax(jnp.concatenate([s_past, s_new], axis=-1), axis=-1)
    att = p[:, :, 0] - lam * p[:, :, 1]
    o = jnp.einsum('bhqk,bkhd->bqhd', att[..., :p_len], past_v.astype(jnp.float32))
    o = o + jnp.einsum('bhqk,bkhd->bqhd', att[..., p_len:], v_new.astype(jnp.float32))
    return o.astype(q.dtype)


def setup_inputs(seed: int = 0) -> dict:
    key = jax.random.key(seed)
    ks = iter(jax.random.split(key, 64))
    f32 = jnp.float32

    def nrm(shape, scale=1.0):
        return jax.random.normal(next(ks), shape, f32) * scale

    def gain(shape):
        return 1.0 + nrm(shape, 0.02)

    n_pages = PAST_LEN // PAGE_SIZE
    n_used = DEC_BATCH * n_pages
    n_pool = n_used + max(1, n_used // 4)
    perm = jax.random.permutation(next(ks), n_pool).astype(jnp.int32)
    page_table = perm[:n_used].reshape(DEC_BATCH, n_pages)
    d = D_MODEL
    return {
        'x_prompt': nrm((BATCH, SEQ, d)),
        'x_sample': nrm((DEC_BATCH, DEC_SEQ, d)),
        'mem_prompt': nrm((BATCH, MEM_TOKENS, d)),
        'state_wkv': nrm((N_A, DEC_BATCH, RWKV_HEADS, RWKV_N, RWKV_N), 0.5),
        'state_shift': nrm((N_A, DEC_BATCH, d)),
        'cache_mem_k': nrm((DEPTH, DEC_BATCH, MEM_TOKENS, MEM_HEADS, MEM_DH)),
        'cache_mem_v': nrm((DEPTH, DEC_BATCH, MEM_TOKENS, MEM_HEADS, MEM_DH)),
        'cache_k': nrm((n_pool, PAGE_SIZE, DIFF_HEADS, 2, DIFF_DH)),
        'cache_v': nrm((n_pool, PAGE_SIZE, DIFF_HEADS, DIFF_DV)),
        'page_table': page_table,
        'ffn_norm': gain((DEPTH, 2, d)),
        'ffn_w13': nrm((DEPTH, 2, d, 2 * D_FF), d ** -0.5),
        'ffn_w2': nrm((DEPTH, 2, D_FF, d), D_FF ** -0.5),
        'mix_norm': gain((DEPTH, d)),
        'w_out': nrm((DEPTH, MIX_W, d), MIX_W ** -0.5),
        'mem_norm': gain((DEPTH, d)),
        'mem_w_kv': nrm((DEPTH, d, 2 * MEM_W), d ** -0.5),
        'mem_q_norm': gain((DEPTH, MEM_DH)),
        'mem_k_norm': gain((DEPTH, MEM_DH)),
        'a_w_in': nrm((N_A, 2 * d, A_IN), (2 * d) ** -0.5),
        'a_mu': jax.random.uniform(next(ks), (N_A, 3, d), f32),
        'a_w0': jax.random.uniform(next(ks), (N_A, RWKV_W), f32, -6.0, 1.0),
        'a_w1': nrm((N_A, d, LORA_W), d ** -0.5),
        'a_w2': nrm((N_A, LORA_W, RWKV_W), 0.5 * LORA_W ** -0.5),
        'a_a0': nrm((N_A, RWKV_W), 0.1),
        'a_a1': nrm((N_A, d, LORA_A), d ** -0.5),
        'a_a2': nrm((N_A, LORA_A, RWKV_W), 0.5 * LORA_A ** -0.5),
        'a_g1': nrm((N_A, d, LORA_G), d ** -0.5),
        'a_g2': nrm((N_A, LORA_G, RWKV_W), LORA_G ** -0.5),
        'a_k_k': 0.85 + nrm((N_A, RWKV_W), 0.02),
        'a_k_a': gain((N_A, RWKV_W)),
        'a_r_k': nrm((N_A, RWKV_HEADS, RWKV_N), 0.1),
        'a_lnx_w': gain((N_A, RWKV_W)),
        'a_lnx_b': nrm((N_A, RWKV_W), 0.02),
        'kv_norm': gain((d,)),
        'kv_w': nrm((d, 2 * DIFF_W), d ** -0.5),
        'k_norm': gain((DIFF_DH,)),
        'b_w_in': nrm((N_B, d, DIFF_W + MEM_W), d ** -0.5),
        'b_q_norm': gain((N_B, DIFF_DH)),
        'b_lam': nrm((N_B, 4, DIFF_DH), 0.1),
        'b_subln': gain((N_B, DIFF_DV)),
    }


def reference(x_prompt, x_sample, mem_prompt, state_wkv, state_shift, cache_mem_k, cache_mem_v, cache_k, cache_v, page_table,
              ffn_norm, ffn_w13, ffn_w2, mix_norm, w_out, mem_norm, mem_w_kv, mem_q_norm, mem_k_norm,
              a_w_in, a_mu, a_w0, a_w1, a_w2, a_a0, a_a1, a_a2, a_g1, a_g2, a_k_k, a_k_a, a_r_k, a_lnx_w, a_lnx_b,
              kv_norm, kv_w, k_norm, b_w_in, b_q_norm, b_lam, b_subln):
    f32 = jnp.float32

    def ffn(x, i, j):
        return swiglu_half(x, ffn_norm[i, j], ffn_w13[i, j], ffn_w2[i, j])

    def layer_a(x, i, shift_prev, s0, mk, mv):
        x = ffn(x, i, 0)
        xn = rms_norm(x, mix_norm[i])
        y, q_mem, s_new = rwkv7_mix(xn, shift_prev, s0, a_w_in[i], a_mu[i], a_w0[i], a_w1[i], a_w2[i],
                                    a_a0[i], a_a1[i], a_a2[i], a_g1[i], a_g2[i], a_k_k[i], a_k_a[i],
                                    a_r_k[i], a_lnx_w[i], a_lnx_b[i])
        o_mem = memory_attend(q_mem, mk, mv, mem_q_norm[i])
        x = x + jnp.concatenate([y, o_mem], axis=-1) @ w_out[i]
        return ffn(x, i, 1), xn[:, -1], s_new

    def layer_b(x, j, cos, sin, mk, mv, attend):
        i = N_A + j
        b, t, _ = x.shape
        x = ffn(x, i, 0)
        xn = rms_norm(x, mix_norm[i])
        q, q_mem = jnp.split(xn @ b_w_in[j], [DIFF_W], axis=-1)
        q = partial_rope(rms_norm(q.reshape(b, t, DIFF_HEADS, 2, DIFF_DH), b_q_norm[j]), cos, sin)
        lam_init = 0.8 - 0.6 * math.exp(-0.3 * i)
        lq = b_lam[j].astype(f32)
        lam = jnp.exp(jnp.sum(lq[0] * lq[1])) - jnp.exp(jnp.sum(lq[2] * lq[3])) + lam_init
        o = rms_norm(attend(q, lam), b_subln[j], SUBLN_EPS) * (1.0 - lam_init)
        o_mem = memory_attend(q_mem, mk, mv, mem_q_norm[i])
        x = x + jnp.concatenate([o.reshape(b, t, DIFF_W), o_mem], axis=-1) @ w_out[i]
        return ffn(x, i, 1)

    mk_p, mv_p = [], []
    for i in range(DEPTH):
        mk, mv = memory_kv(mem_prompt, mem_norm[i], mem_w_kv[i], mem_k_norm[i])
        mk_p.append(mk)
        mv_p.append(mv)
    x = x_prompt
    wkv_p, shift_p = [], []
    for i in range(N_A):
        x, sh, s = layer_a(x, i, jnp.zeros((BATCH, D_MODEL), x.dtype),
                           jnp.zeros((BATCH, RWKV_HEADS, RWKV_N, RWKV_N), f32), mk_p[i], mv_p[i])
        wkv_p.append(s)
        shift_p.append(sh)
    cos_p, sin_p = rope_tables(jnp.arange(SEQ))
    k_rows_p, v_rows_p = shared_kv(x, kv_norm, kv_w, k_norm, cos_p, sin_p)
    attend_p = lambda q, lam: diff_attention_prompt(q, k_rows_p, v_rows_p, lam)
    for j in range(N_B):
        x = layer_b(x, j, cos_p, sin_p, mk_p[N_A + j], mv_p[N_A + j], attend_p)
    y_prompt = x

    x = x_sample
    wkv_s, shift_s = [], []
    for i in range(N_A):
        x, sh, s = layer_a(x, i, state_shift[i], state_wkv[i], cache_mem_k[i], cache_mem_v[i])
        wkv_s.append(s)
        shift_s.append(sh)
    cos_s, sin_s = rope_tables(PAST_LEN + jnp.arange(DEC_SEQ))
    k_rows_s, v_rows_s = shared_kv(x, kv_norm, kv_w, k_norm, cos_s, sin_s)
    n_pages = PAST_LEN // PAGE_SIZE
    past_k = jnp.take(cache_k, page_table, axis=0).reshape(DEC_BATCH, n_pages * PAGE_SIZE, DIFF_HEADS, 2, DIFF_DH)
    past_v = jnp.take(cache_v, page_table, axis=0).reshape(DEC_BATCH, n_pages * PAGE_SIZE, DIFF_HEADS, DIFF_DV)
    attend_s = lambda q, lam: diff_attention_sample(q, past_k, past_v, k_rows_s, v_rows_s, lam)
    for j in range(N_B):
        x = layer_b(x, j, cos_s, sin_s, cache_mem_k[N_A + j], cache_mem_v[N_A + j], attend_s)
    y_sample = x

    wkv_prompt = jnp.stack(wkv_p)
    shift_prompt = jnp.stack(shift_p)
    wkv_sample = jnp.stack(wkv_s)
    shift_sample = jnp.stack(shift_s)
    mem_k_prompt = jnp.stack(mk_p)
    mem_v_prompt = jnp.stack(mv_p)
    return (y_prompt, y_sample, wkv_prompt, shift_prompt, wkv_sample, shift_sample,
            k_rows_p, v_rows_p, k_rows_s, v_rows_s, mem_k_prompt, mem_v_prompt)
```

```python
import functools
import math

import jax
import jax.numpy as jnp
from jax import lax
from jax.experimental import pallas as pl
from jax.experimental.pallas import tpu as pltpu

F32 = jnp.float32
BF16 = jnp.bfloat16

D_MODEL = 2048
DEPTH = 4
N_A = 2
N_B = 2
MEM_TOKENS = 256
MEM_HEADS = 4
MEM_W = 512
MEM_DH = 128
RWKV_W = 1536
RWKV_N = 64
RWKV_HEADS = 24
LORA_PAD = 128
LORA_G = 256
A_IN = 3 * RWKV_W + MEM_W
DIFF_W = 1536
DIFF_DV = 128
DIFF_HEADS = 12
DIFF_DH = 64
ROT_DIM = 16
ROPE_THETA = 500000.0
D_FF = 5632
PAST_LEN = 16384
PAGE_SIZE = 128
NORM_EPS = 1e-6
LNX_EPS = 64e-5
SUBLN_EPS = 1e-5

LANES = 128
CHUNK = 64
VMEM_LIMIT = 56 * 1024 * 1024

NT_DIMS = (((1,), (1,)), ((), ()))
TN_DIMS = (((0,), (0,)), ((), ()))


def _params(*sem):
    return pltpu.CompilerParams(dimension_semantics=sem, vmem_limit_bytes=VMEM_LIMIT)


def _dot(a, b, dims=None):
    if dims is None:
        return jnp.dot(a, b, preferred_element_type=F32)
    return lax.dot_general(a, b, dims, preferred_element_type=F32)


def _rms(x, g, eps):
    return x * lax.rsqrt(jnp.mean(x * x, axis=-1, keepdims=True) + eps) * g


def _sigmoid(x):
    return 1.0 / (1.0 + jnp.exp(-x))


def _split2(x):
    hi = x.astype(BF16)
    lo = (x - hi.astype(F32)).astype(BF16)
    return hi, lo


def _group_matrix(scale):
    r = lax.broadcasted_iota(jnp.int32, (LANES, LANES), 0) // RWKV_N
    c = lax.broadcasted_iota(jnp.int32, (LANES, LANES), 1) // RWKV_N
    return jnp.where(r == c, scale, 0.0).astype(BF16)


def _group_sum(x, gm):
    hi, lo = _split2(x)
    return _dot(hi, gm) + _dot(lo, gm)


def _ffn_kernel(x_ref, g_ref, w1_ref, w3_ref, w2_ref, o_ref, h_ref, acc_ref):
    f = pl.program_id(1)

    @pl.when(f == 0)
    def _():
        h_ref[...] = _rms(x_ref[...], g_ref[...], NORM_EPS).astype(BF16)
        acc_ref[...] = jnp.zeros_like(acc_ref)

    h = h_ref[...]
    gate = _dot(h, w1_ref[...])
    up = _dot(h, w3_ref[...])
    act = (gate * _sigmoid(gate) * up).astype(BF16)
    acc_ref[...] += _dot(act, w2_ref[...])

    @pl.when(f == pl.num_programs(1) - 1)
    def _():
        o_ref[...] = x_ref[...] + 0.5 * acc_ref[...]


def ffn(x, g, w13, w2, tm, tf=512):
    m, d = x.shape
    nf = D_FF // tf
    return pl.pallas_call(
        _ffn_kernel,
        grid=(m // tm, nf),
        in_specs=[
            pl.BlockSpec((tm, d), lambda i, f: (i, 0)),
            pl.BlockSpec((1, d), lambda i, f: (0, 0)),
            pl.BlockSpec((d, tf), lambda i, f: (0, f)),
            pl.BlockSpec((d, tf), lambda i, f: (0, f + nf)),
            pl.BlockSpec((tf, d), lambda i, f: (f, 0)),
        ],
        out_specs=pl.BlockSpec((tm, d), lambda i, f: (i, 0)),
        out_shape=jax.ShapeDtypeStruct((m, d), F32),
        scratch_shapes=[pltpu.VMEM((tm, d), BF16), pltpu.VMEM((tm, d), F32)],
        compiler_params=_params("parallel", "arbitrary"),
        name="ffn",
    )(x, g.reshape(1, d), w13, w13, w2)


def _norm_mm_kernel(*refs, qk_epilogue):
    if qk_epilogue:
        x_ref, g_ref, w_ref, gh_ref, cos_ref, s1_ref, s2_ref, o_ref, h_ref = refs
    else:
        x_ref, g_ref, w_ref, o_ref, h_ref = refs

    @pl.when(pl.program_id(1) == 0)
    def _():
        h_ref[...] = _rms(x_ref[...], g_ref[...], NORM_EPS).astype(BF16)

    y = _dot(h_ref[...], w_ref[...])
    if not qk_epilogue:
        o_ref[...] = y
        return
    gm = _group_matrix(1.0 / DIFF_DH)
    gh, cos, s1, s2 = gh_ref[...], cos_ref[...], s1_ref[...], s2_ref[...]
    for j in range(y.shape[1] // LANES):
        blk = y[:, j * LANES:(j + 1) * LANES]
        nb = blk * lax.rsqrt(_group_sum(blk * blk, gm) + NORM_EPS) * gh
        half = ROT_DIM // 2
        o_ref[:, j * LANES:(j + 1) * LANES] = (
            nb * cos + pltpu.roll(nb, LANES - half, 1) * s1 + pltpu.roll(nb, half, 1) * s2)


def norm_mm(x, g, w, tm, tn, rope=None, head_gain=None, rows_per_seq=None):
    m, d = x.shape
    n = w.shape[1]
    qk = rope is not None
    in_specs = [
        pl.BlockSpec((tm, d), lambda i, j: (i, 0)),
        pl.BlockSpec((1, d), lambda i, j: (0, 0)),
        pl.BlockSpec((d, tn), lambda i, j: (0, j)),
    ]
    args = [x, g.reshape(1, d), w]
    if qk:
        tiles_per_seq = rows_per_seq // tm
        in_specs.append(pl.BlockSpec((1, LANES), lambda i, j: (0, 0)))
        args.append(jnp.tile(head_gain.reshape(1, DIFF_DH), (1, 2)))
        for t in rope:
            in_specs.append(pl.BlockSpec((tm, LANES), lambda i, j: (i % tiles_per_seq, 0)))
            args.append(t)
    return pl.pallas_call(
        functools.partial(_norm_mm_kernel, qk_epilogue=qk),
        grid=(m // tm, n // tn),
        in_specs=in_specs,
        out_specs=pl.BlockSpec((tm, tn), lambda i, j: (i, j)),
        out_shape=jax.ShapeDtypeStruct((m, n), F32),
        scratch_shapes=[pltpu.VMEM((tm, d), BF16)],
        compiler_params=_params("parallel", "arbitrary"),
        name="norm_mm_qk" if qk else "norm_mm",
    )(*args)


def rope_tables(pos):
    half = ROT_DIM // 2
    inv = ROPE_THETA ** (-jnp.arange(0, ROT_DIM, 2, dtype=F32) / ROT_DIM)
    ang = pos.astype(F32)[:, None] * inv[None, :]
    cos, sin = jnp.cos(ang), jnp.sin(ang)
    t = pos.shape[0]
    rest = DIFF_DH - ROT_DIM
    c64 = jnp.concatenate([cos, cos, jnp.ones((t, rest), F32)], axis=1)
    s1_64 = jnp.concatenate([-sin, jnp.zeros((t, DIFF_DH - half), F32)], axis=1)
    s2_64 = jnp.concatenate([jnp.zeros((t, half), F32), sin, jnp.zeros((t, rest), F32)], axis=1)
    return tuple(jnp.tile(z, (1, 2)) for z in (c64, s1_64, s2_64))


def _mem_kv_kernel(x_ref, g_ref, w_ref, gk_ref, k_ref, v_ref):
    h = _rms(x_ref[...], g_ref[...], NORM_EPS).astype(BF16)
    y = _dot(h, w_ref[...])
    gk = gk_ref[...]
    for j in range(MEM_HEADS):
        blk = y[:, j * MEM_DH:(j + 1) * MEM_DH]
        k_ref[:, j * MEM_DH:(j + 1) * MEM_DH] = _rms(blk, gk, NORM_EPS)
    v_ref[...] = y[:, MEM_W:]


def mem_kv(mem, g, w, gk):
    m, d = mem.shape
    return pl.pallas_call(
        _mem_kv_kernel,
        grid=(1,),
        in_specs=[
            pl.BlockSpec((m, d), lambda i: (0, 0)),
            pl.BlockSpec((1, d), lambda i: (0, 0)),
            pl.BlockSpec((d, 2 * MEM_W), lambda i: (0, 0)),
            pl.BlockSpec((1, MEM_DH), lambda i: (0, 0)),
        ],
        out_specs=[pl.BlockSpec((m, MEM_W), lambda i: (0, 0))] * 2,
        out_shape=[jax.ShapeDtypeStruct((m, MEM_W), F32)] * 2,
        compiler_params=_params("arbitrary"),
        name="mem_kv",
    )(mem, g.reshape(1, d), w, gk.reshape(1, MEM_DH))


def _mem_attn_kernel(q_ref, k_ref, v_ref, gq_ref, o_ref):
    q = q_ref[0]
    rows = q.shape[0]
    if rows < 8:
        q = jnp.broadcast_to(q, (8, q.shape[1]))
    k = k_ref[0]
    v = v_ref[0]
    gq = gq_ref[...]
    for h in range(MEM_HEADS):
        sl = slice(h * MEM_DH, (h + 1) * MEM_DH)
        qh = _rms(q[:, sl], gq, NORM_EPS).astype(BF16)
        s = _dot(qh, k[:, sl].astype(BF16), NT_DIMS) * (MEM_DH ** -0.5)
        p = jnp.exp(s - jnp.max(s, axis=-1, keepdims=True))
        o = _dot(p.astype(BF16), v[:, sl].astype(BF16)) / jnp.sum(p, axis=-1, keepdims=True)
        o_ref[0, :, sl] = o[:rows].astype(o_ref.dtype)


def mem_attn(proj, q_col_block, mk, mv, gq, tq):
    b, t, _ = proj.shape
    return pl.pallas_call(
        _mem_attn_kernel,
        grid=(b, t // tq),
        in_specs=[
            pl.BlockSpec((1, tq, MEM_W), lambda i, j: (i, j, q_col_block)),
            pl.BlockSpec((1, MEM_TOKENS, MEM_W), lambda i, j: (i, 0, 0)),
            pl.BlockSpec((1, MEM_TOKENS, MEM_W), lambda i, j: (i, 0, 0)),
            pl.BlockSpec((1, MEM_DH), lambda i, j: (0, 0)),
        ],
        out_specs=pl.BlockSpec((1, tq, MEM_W), lambda i, j: (i, j, 0)),
        out_shape=jax.ShapeDtypeStruct((b, t, MEM_W), BF16),
        compiler_params=_params("parallel", "arbitrary"),
        name="mem_attn",
    )(proj, mk, mv, gq.reshape(1, MEM_DH))


def _out_mm_kernel(x_ref, a_ref, b_ref, wa_ref, wb_ref, o_ref):
    o_ref[...] = x_ref[...] + _dot(a_ref[...], wa_ref[...]) + _dot(b_ref[...], wb_ref[...])


def out_mm(x, a, b, w, tm, tn=512):
    m, d = x.shape
    ka, kb = a.shape[1], b.shape[1]
    kb_blocks = ka // kb
    return pl.pallas_call(
        _out_mm_kernel,
        grid=(m // tm, d // tn),
        in_specs=[
            pl.BlockSpec((tm, tn), lambda i, j: (i, j)),
            pl.BlockSpec((tm, ka), lambda i, j: (i, 0)),
            pl.BlockSpec((tm, kb), lambda i, j: (i, 0)),
            pl.BlockSpec((ka, tn), lambda i, j: (0, j)),
            pl.BlockSpec((kb, tn), lambda i, j: (kb_blocks, j)),
        ],
        out_specs=pl.BlockSpec((tm, tn), lambda i, j: (i, j)),
        out_shape=jax.ShapeDtypeStruct((m, d), F32),
        compiler_params=_params("parallel", "arbitrary"),
        name="out_mm",
    )(x, a, b, w, w)


def _a_prep_kernel(x_ref, xs_ref, sp_ref, g_ref, mu_ref, xn_ref, cat_ref, mix_ref, *, tiles_per_seq):
    g = g_ref[...]
    xn = _rms(x_ref[...], g, NORM_EPS)
    if tiles_per_seq is None:
        xp = sp_ref[...]
    else:
        xp = _rms(xs_ref[...], g, NORM_EPS)
        first = pl.program_id(0) % tiles_per_seq == 0
        row = lax.broadcasted_iota(jnp.int32, xn.shape, 0)
        xp = jnp.where(jnp.logical_and(row == 0, first), sp_ref[0], xp)
    xx = xp - xn
    d = xn.shape[1]
    xn_ref[...] = xn
    cat_ref[:, :d] = xn.astype(BF16)
    cat_ref[:, d:] = xx.astype(BF16)
    for i in range(3):
        mix_ref[:, i * d:(i + 1) * d] = (xn + xx * mu_ref[i:i + 1, :]).astype(BF16)


def a_prep(x, x_shift, shift_prev, g, mu, tm, rows_per_seq):
    m, d = x.shape
    if rows_per_seq == 1:
        tiles_per_seq = None
        x_shift = x
        sp, sp_spec = shift_prev, pl.BlockSpec((tm, d), lambda i: (i, 0))
    else:
        tiles_per_seq = rows_per_seq // tm
        sp = shift_prev.reshape(shift_prev.shape[0], 1, d)
        sp_spec = pl.BlockSpec((1, 1, d), lambda i: (i // tiles_per_seq, 0, 0))
    return pl.pallas_call(
        functools.partial(_a_prep_kernel, tiles_per_seq=tiles_per_seq),
        grid=(m // tm,),
        in_specs=[
            pl.BlockSpec((tm, d), lambda i: (i, 0)),
            pl.BlockSpec((tm, d), lambda i: (i, 0)),
            sp_spec,
            pl.BlockSpec((1, d), lambda i: (0, 0)),
            pl.BlockSpec((3, d), lambda i: (0, 0)),
        ],
        out_specs=[
            pl.BlockSpec((tm, d), lambda i: (i, 0)),
            pl.BlockSpec((tm, 2 * d), lambda i: (i, 0)),
            pl.BlockSpec((tm, 3 * d), lambda i: (i, 0)),
        ],
        out_shape=[
            jax.ShapeDtypeStruct((m, d), F32),
            jax.ShapeDtypeStruct((m, 2 * d), BF16),
            jax.ShapeDtypeStruct((m, 3 * d), BF16),
        ],
        compiler_params=_params("parallel"),
        name="a_prep",
    )(x, x_shift, sp, g.reshape(1, d), mu)


def _mm_kernel(a_ref, w_ref, o_ref):
    o_ref[...] = _dot(a_ref[...], w_ref[...])


def mm(a, w, tm, tn=512):
    m, k = a.shape
    n = w.shape[1]
    return pl.pallas_call(
        _mm_kernel,
        grid=(m // tm, n // tn),
        in_specs=[pl.BlockSpec((tm, k), lambda i, j: (i, 0)), pl.BlockSpec((k, tn), lambda i, j: (0, j))],
        out_specs=pl.BlockSpec((tm, tn), lambda i, j: (i, j)),
        out_shape=jax.ShapeDtypeStruct((m, n), F32),
        compiler_params=_params("parallel", "arbitrary"),
        name="mm",
    )(a, w)


def _a_mix_kernel(k_ref, mix_ref, w1_ref, w2_ref, a1_ref, a2_ref, g1_ref, g2_ref, w0_ref, a0_ref, kk_ref,
                  ka_ref, ld_ref, kp_ref, kn_ref, b_ref, g_ref):
    d = D_MODEL
    xw, xa, xg = mix_ref[:, :d], mix_ref[:, d:2 * d], mix_ref[:, 2 * d:]
    wl = _dot(jnp.tanh(_dot(xw, w1_ref[...])).astype(BF16), w2_ref[...]) + w0_ref[...]
    w = -(jnp.maximum(-wl, 0.0) + jnp.log(1.0 + jnp.exp(-jnp.abs(wl)))) - 0.5
    ld_ref[...] = -jnp.exp(w)
    a = _sigmoid(_dot(_dot(xa, a1_ref[...]).astype(BF16), a2_ref[...]) + a0_ref[...])
    g_ref[...] = _dot(_sigmoid(_dot(xg, g1_ref[...])).astype(BF16), g2_ref[...])
    k = k_ref[...]
    kp_ref[...] = k * (1.0 + (a - 1.0) * ka_ref[...])
    kraw = k * kk_ref[...]
    gm = _group_matrix(1.0)
    for j in range(RWKV_W // LANES):
        sl = slice(j * LANES, (j + 1) * LANES)
        blk = kraw[:, sl]
        kn = blk / jnp.maximum(jnp.sqrt(_group_sum(blk * blk, gm)), 1e-12)
        kn_ref[:, sl] = kn
        b_ref[:, sl] = kn * a[:, sl]


def a_mix(proj, mix, lw, tm):
    m = proj.shape[0]
    row = lambda i: (i, 0)
    fixed = lambda i: (0, 0)
    vec = pl.BlockSpec((1, RWKV_W), fixed)
    out = jax.ShapeDtypeStruct((m, RWKV_W), F32)
    return pl.pallas_call(
        _a_mix_kernel,
        grid=(m // tm,),
        in_specs=[
            pl.BlockSpec((tm, RWKV_W), lambda i: (i, 1)),
            pl.BlockSpec((tm, 3 * D_MODEL), row),
            pl.BlockSpec((D_MODEL, LORA_PAD), fixed), pl.BlockSpec((LORA_PAD, RWKV_W), fixed),
            pl.BlockSpec((D_MODEL, LORA_PAD), fixed), pl.BlockSpec((LORA_PAD, RWKV_W), fixed),
            pl.BlockSpec((D_MODEL, LORA_G), fixed), pl.BlockSpec((LORA_G, RWKV_W), fixed),
            vec, vec, vec, vec,
        ],
        out_specs=[pl.BlockSpec((tm, RWKV_W), row)] * 5,
        out_shape=[out] * 5,
        compiler_params=_params("parallel"),
        name="a_mix",
    )(proj, mix, lw["w1"], lw["w2"], lw["a1"], lw["a2"], lw["g1"], lw["g2"],
      lw["w0"], lw["a0"], lw["k_k"], lw["k_a"])


def _wkv_chunk(s, r, ld, k, v, kn, b):
    c = r.shape[0]
    c2 = 2 * c
    lane = lax.broadcasted_iota(jnp.int32, (c, LANES), 1)
    in_head0 = lane < RWKV_N

    def stack(z):
        return jnp.concatenate([jnp.where(in_head0, z, 0.0), jnp.where(in_head0, 0.0, z)], axis=0)

    ti = lax.broadcasted_iota(jnp.int32, (c, c), 0)
    tj = lax.broadcasted_iota(jnp.int32, (c, c), 1)
    tri = jnp.where(ti >= tj, 1.0, 0.0).astype(BF16)
    p1 = ld.astype(BF16)
    rem = ld - p1.astype(F32)
    p2 = rem.astype(BF16)
    p3 = (rem - p2.astype(F32)).astype(BF16)
    cum = _dot(tri, p1) + _dot(tri, p2) + _dot(tri, p3)
    eg = jnp.exp(cum)
    einv = jnp.exp(-cum)
    at_s = stack(-kn * jnp.exp(cum - ld)).astype(BF16)
    rt_s = stack(r * eg).astype(BF16)
    bt = (b * einv).astype(BF16)
    kt = (k * einv).astype(BF16)
    v_s = stack(v).astype(BF16)
    ones = jnp.ones((c, LANES), BF16)
    g_rows = jnp.exp(_dot(p1, ones, TN_DIMS) + _dot(p2, ones, TN_DIMS) + _dot(p3, ones, TN_DIMS))

    gmat = _dot(jnp.concatenate([at_s, rt_s], axis=0), jnp.concatenate([bt, bt, kt, kt], axis=0), NT_DIMS)
    ri = lax.broadcasted_iota(jnp.int32, (c2, c2), 0)
    ci = lax.broadcasted_iota(jnp.int32, (c2, c2), 1)
    same = (ri // c) == (ci // c)
    strict = jnp.logical_and(same, (ri % c) > (ci % c))
    incl = jnp.logical_and(same, (ri % c) >= (ci % c))
    a_ab = jnp.where(strict, gmat[:c2, :c2], 0.0)
    a_ak = jnp.where(strict, gmat[:c2, c2:], 0.0).astype(BF16)
    a_rb = jnp.where(incl, gmat[c2:, :c2], 0.0).astype(BF16)
    a_rk = jnp.where(incl, gmat[c2:, c2:], 0.0).astype(BF16)

    def bdot(x, y):
        return _dot(x.astype(BF16), y.astype(BF16))

    ad = jnp.where((ri // 8) == (ci // 8), a_ab, 0.0)
    ad2 = bdot(ad, ad)
    ad4 = bdot(ad2, ad2)
    inv = jnp.where(ri == ci, 1.0, 0.0) + ad
    inv = inv + bdot(inv, ad2)
    inv = inv + bdot(inv, ad4)
    size = 8
    while size < c:
        lower_left = jnp.logical_and(
            (ri // (2 * size)) == (ci // (2 * size)),
            jnp.logical_and((ri // size) % 2 == 1, (ci // size) % 2 == 0))
        inv = inv + bdot(bdot(inv, jnp.where(lower_left, a_ab, 0.0)), inv)
        size *= 2

    s_b = s.astype(BF16)
    rhs = _dot(jnp.concatenate([a_ak, at_s], axis=1), jnp.concatenate([v_s, s_b], axis=0))
    u = _dot(inv.astype(BF16), rhs.astype(BF16)).astype(BF16)
    y_s = _dot(jnp.concatenate([rt_s, a_rb, a_rk], axis=1), jnp.concatenate([s_b, u, v_s], axis=0))
    y = y_s[:c] + y_s[c:]
    ds = _dot(jnp.concatenate([stack(b * einv).astype(BF16), stack(k * einv).astype(BF16)], axis=0),
              jnp.concatenate([u, v_s], axis=0), TN_DIMS)
    return y, (s + ds) * g_rows


def _wkv_kernel(r_ref, ld_ref, k_ref, v_ref, kn_ref, b_ref, y_ref, s_ref, st_ref, *, chunks):
    @pl.when(pl.program_id(2) == 0)
    def _():
        st_ref[...] = jnp.zeros_like(st_ref)

    s = st_ref[...]
    for i in range(chunks):
        sl = slice(i * CHUNK, (i + 1) * CHUNK)
        y, s = _wkv_chunk(s, r_ref[0, sl, :], ld_ref[0, sl, :], k_ref[0, sl, :], v_ref[0, sl, :],
                          kn_ref[0, sl, :], b_ref[0, sl, :])
        y_ref[0, sl, :] = y
    st_ref[...] = s

    @pl.when(pl.program_id(2) == pl.num_programs(2) - 1)
    def _():
        s_ref[0, 0] = s


def wkv_scan(proj, ld, kp, kn, b, chunks=1):
    bsz, t, _ = proj.shape
    pairs = RWKV_W // LANES
    tb = chunks * CHUNK
    blk = lambda off: pl.BlockSpec((1, tb, LANES), lambda i, p, c: (i, c, p + off))
    return pl.pallas_call(
        functools.partial(_wkv_kernel, chunks=chunks),
        grid=(bsz, pairs, t // tb),
        in_specs=[blk(0), blk(0), blk(0), blk(2 * pairs), blk(0), blk(0)],
        out_specs=[
            pl.BlockSpec((1, tb, LANES), lambda i, p, c: (i, c, p)),
            pl.BlockSpec((1, 1, LANES, LANES), lambda i, p, c: (i, p, 0, 0)),
        ],
        out_shape=[
            jax.ShapeDtypeStruct((bsz, t, RWKV_W), F32),
            jax.ShapeDtypeStruct((bsz, pairs, LANES, LANES), F32),
        ],
        scratch_shapes=[pltpu.VMEM((LANES, LANES), F32)],
        compiler_params=_params("parallel", "parallel", "arbitrary"),
        name="wkv_scan",
    )(proj, ld, kp, proj, kn, b)


def _wkv_step_kernel(s_ref, r_ref, ld_ref, k_ref, v_ref, kn_ref, b_ref, y_ref, so_ref):
    s = s_ref[0]
    sa = jnp.sum(s * (-kn_ref[0]), axis=-1, keepdims=True)
    s = s * jnp.exp(ld_ref[0]) + sa * b_ref[0] + v_ref[0] * k_ref[0]
    so_ref[0] = s
    y_ref[0] = jnp.sum(s * r_ref[0], axis=-1, keepdims=True)


def wkv_step(s0, r, ld, k, v, kn, b):
    bsz = s0.shape[0]
    h, n = RWKV_HEADS, RWKV_N
    st = pl.BlockSpec((1, h, n, n), lambda i: (i, 0, 0, 0))
    rw = pl.BlockSpec((1, h, 1, n), lambda i: (i, 0, 0, 0))
    cl = pl.BlockSpec((1, h, n, 1), lambda i: (i, 0, 0, 0))
    return pl.pallas_call(
        _wkv_step_kernel,
        grid=(bsz,),
        in_specs=[st, rw, rw, rw, cl, rw, rw],
        out_specs=[cl, st],
        out_shape=[jax.ShapeDtypeStruct((bsz, h, n, 1), F32), jax.ShapeDtypeStruct((bsz, h, n, n), F32)],
        compiler_params=_params("parallel"),
        name="wkv_step",
    )(s0, r, ld, k, v, kn, b)


def _a_post_kernel(y_ref, r_ref, kp_ref, v_ref, g_ref, lw_ref, lb_ref, rk_ref, o_ref):
    gsum = _group_matrix(1.0)
    gmean = _group_matrix(1.0 / RWKV_N)
    for j in range(RWKV_W // LANES):
        sl = slice(j * LANES, (j + 1) * LANES)
        y = y_ref[:, sl]
        cen = y - _group_sum(y, gmean)
        yn = cen * lax.rsqrt(_group_sum(cen * cen, gmean) + LNX_EPS) * lw_ref[:, sl] + lb_ref[:, sl]
        bonus = _group_sum(r_ref[:, sl] * kp_ref[:, sl] * rk_ref[:, sl], gsum) * v_ref[:, sl]
        o_ref[:, sl] = ((yn + bonus) * g_ref[:, sl]).astype(o_ref.dtype)


def a_post(y, proj, kp, g, lnx_w, lnx_b, r_k, tm):
    m = y.shape[0]
    row = lambda i: (i, 0)
    vec = pl.BlockSpec((1, RWKV_W), lambda i: (0, 0))
    tile = pl.BlockSpec((tm, RWKV_W), row)
    return pl.pallas_call(
        _a_post_kernel,
        grid=(m // tm,),
        in_specs=[tile, pl.BlockSpec((tm, RWKV_W), lambda i: (i, 0)), tile,
                  pl.BlockSpec((tm, RWKV_W), lambda i: (i, 2)), tile, vec, vec, vec],
        out_specs=tile,
        out_shape=jax.ShapeDtypeStruct((m, RWKV_W), BF16),
        compiler_params=_params("parallel"),
        name="a_post",
    )(y, proj, kp, proj, g, lnx_w.reshape(1, RWKV_W), lnx_b.reshape(1, RWKV_W), r_k.reshape(1, RWKV_W))


def _lambda(lam_ref, lam_init):
    lq = lam_ref[...]
    l1 = jnp.sum(lq[0:1] * lq[1:2], axis=-1, keepdims=True)
    l2 = jnp.sum(lq[2:3] * lq[3:4], axis=-1, keepdims=True)
    return jnp.exp(l1) - jnp.exp(l2) + lam_init


def _diff_attn_kernel(q_ref, k_ref, v_ref, lam_ref, gs_ref, o_ref, m_ref, l_ref, acc_ref, *, tq, tk, lam_init):
    qi = pl.program_id(2)
    ki = pl.program_id(3)

    @pl.when(ki == 0)
    def _():
        m_ref[...] = jnp.full_like(m_ref, -jnp.inf)
        l_ref[...] = jnp.zeros_like(l_ref)
        acc_ref[...] = jnp.zeros_like(acc_ref)

    @pl.when(ki * tk <= qi * tq + (tq - 1))
    def _():
        q = q_ref[0] * (DIFF_DH ** -0.5)
        kb = k_ref[0].astype(BF16)
        vb = v_ref[0].astype(BF16)
        lane = lax.broadcasted_iota(jnp.int32, q.shape, 1)
        q_pos = qi * tq + lax.broadcasted_iota(jnp.int32, (tq, tk), 0)
        k_pos = ki * tk + lax.broadcasted_iota(jnp.int32, (tq, tk), 1)
        visible = k_pos <= q_pos
        for c in range(2):
            in_comp = (lane < DIFF_DH) if c == 0 else (lane >= DIFF_DH)
            s = _dot(jnp.where(in_comp, q, 0.0).astype(BF16), kb, NT_DIMS)
            s = jnp.where(visible, s, -jnp.inf)
            m_old = m_ref[c]
            m_new = jnp.maximum(m_old, jnp.max(s, axis=-1, keepdims=True))
            alpha = jnp.exp(m_old - m_new)
            p = jnp.exp(s - m_new)
            l_ref[c] = alpha * l_ref[c] + jnp.sum(p, axis=-1, keepdims=True)
            acc_ref[c] = alpha * acc_ref[c] + _dot(p.astype(BF16), vb)
            m_ref[c] = m_new

    @pl.when(ki == pl.num_programs(3) - 1)
    def _():
        lam = _lambda(lam_ref, lam_init)
        o = acc_ref[0] / l_ref[0] - lam * (acc_ref[1] / l_ref[1])
        o_ref[0] = (_rms(o, gs_ref[...], SUBLN_EPS) * (1.0 - lam_init)).astype(o_ref.dtype)


def diff_attn_prompt(q, k, v, lam_p, subln, lam_init, tq=512, tk=512):
    b, t, _ = q.shape
    kv_map = lambda i, h, qi, ki: (i, jnp.minimum(ki, (qi * tq + tq - 1) // tk), h)
    return pl.pallas_call(
        functools.partial(_diff_attn_kernel, tq=tq, tk=tk, lam_init=lam_init),
        grid=(b, DIFF_HEADS, t // tq, t // tk),
        in_specs=[
            pl.BlockSpec((1, tq, DIFF_DV), lambda i, h, qi, ki: (i, qi, h)),
            pl.BlockSpec((1, tk, DIFF_DV), kv_map),
            pl.BlockSpec((1, tk, DIFF_DV), kv_map),
            pl.BlockSpec((4, DIFF_DH), lambda i, h, qi, ki: (0, 0)),
            pl.BlockSpec((1, DIFF_DV), lambda i, h, qi, ki: (0, 0)),
        ],
        out_specs=pl.BlockSpec((1, tq, DIFF_DV), lambda i, h, qi, ki: (i, qi, h)),
        out_shape=jax.ShapeDtypeStruct((b, t, DIFF_W), BF16),
        scratch_shapes=[pltpu.VMEM((2, tq, 1), F32), pltpu.VMEM((2, tq, 1), F32),
                        pltpu.VMEM((2, tq, DIFF_DV), F32)],
        compiler_params=_params("parallel", "parallel", "parallel", "arbitrary"),
        name="diff_attn_prompt",
    )(q, k, v, lam_p, subln.reshape(1, DIFF_DV))


DEC_ROWS = 32
DEC_PAGES = 4


def _dec_attn_kernel(pt_ref, q_ref, kn_ref, vn_ref, lam_ref, gs_ref, *refs, lam_init):
    k_refs = refs[:DEC_PAGES]
    v_refs = refs[DEC_PAGES:2 * DEC_PAGES]
    o_ref, qr_ref, m_ref, l_ref, acc_ref = refs[2 * DEC_PAGES:]
    step = pl.program_id(1)
    row_group = lax.broadcasted_iota(jnp.int32, (DEC_ROWS, DIFF_W), 0)
    lane_group = lax.broadcasted_iota(jnp.int32, (DEC_ROWS, DIFF_W), 1) // DIFF_DH

    @pl.when(step == 0)
    def _():
        qrows = jnp.where(row_group == lane_group, q_ref[0] * (DIFF_DH ** -0.5), 0.0)
        qr_ref[...] = qrows
        s_new = jnp.sum(qrows * kn_ref[0], axis=-1, keepdims=True)
        m_ref[...] = s_new
        l_ref[...] = jnp.ones_like(l_ref)
        acc_ref[...] = jnp.broadcast_to(vn_ref[0], acc_ref.shape)

    qb = qr_ref[...].astype(BF16)
    for k_ref, v_ref in zip(k_refs, v_refs):
        s = _dot(qb, k_ref[0].astype(BF16), NT_DIMS)
        m_old = m_ref[...]
        m_new = jnp.maximum(m_old, jnp.max(s, axis=-1, keepdims=True))
        alpha = jnp.exp(m_old - m_new)
        p = jnp.exp(s - m_new)
        l_ref[...] = alpha * l_ref[...] + jnp.sum(p, axis=-1, keepdims=True)
        acc_ref[...] = alpha * acc_ref[...] + _dot(p.astype(BF16), v_ref[0].astype(BF16))
        m_ref[...] = m_new

    @pl.when(step == pl.num_programs(1) - 1)
    def _():
        lam = _lambda(lam_ref, lam_init)
        w = acc_ref[...] / l_ref[...]
        head_of_lane = lane_group // 2
        comp0 = jnp.logical_and(row_group % 2 == 0, row_group // 2 == head_of_lane)
        comp1 = jnp.logical_and(row_group % 2 == 1, row_group // 2 == head_of_lane)
        o = (jnp.sum(jnp.where(comp0, w, 0.0), axis=0, keepdims=True)
             - lam * jnp.sum(jnp.where(comp1, w, 0.0), axis=0, keepdims=True))
        gs = gs_ref[...]
        for h in range(DIFF_HEADS):
            sl = slice(h * DIFF_DV, (h + 1) * DIFF_DV)
            o_ref[0, :, sl] = (_rms(o[:, sl], gs, SUBLN_EPS) * (1.0 - lam_init)).astype(o_ref.dtype)


def diff_attn_decode(q, k_new, v_new, cache_k, cache_v, page_table, lam_p, subln, lam_init):
    b = q.shape[0]
    n_pages = page_table.shape[1]
    tok = pl.BlockSpec((1, 1, DIFF_W), lambda i, s, pt: (i, 0, 0))

    def page_spec(j):
        return pl.BlockSpec((1, PAGE_SIZE, DIFF_W), lambda i, s, pt: (pt[i, s * DEC_PAGES + j], 0, 0))

    grid_spec = pltpu.PrefetchScalarGridSpec(
        num_scalar_prefetch=1,
        grid=(b, n_pages // DEC_PAGES),
        in_specs=[tok, tok, tok,
                  pl.BlockSpec((4, DIFF_DH), lambda i, s, pt: (0, 0)),
                  pl.BlockSpec((1, DIFF_DV), lambda i, s, pt: (0, 0))]
        + [page_spec(j) for j in range(DEC_PAGES)] * 2,
        out_specs=tok,
        scratch_shapes=[pltpu.VMEM((DEC_ROWS, DIFF_W), F32), pltpu.VMEM((DEC_ROWS, 1), F32),
                        pltpu.VMEM((DEC_ROWS, 1), F32), pltpu.VMEM((DEC_ROWS, DIFF_W), F32)],
    )
    return pl.pallas_call(
        functools.partial(_dec_attn_kernel, lam_init=lam_init),
        grid_spec=grid_spec,
        out_shape=jax.ShapeDtypeStruct((b, 1, DIFF_W), BF16),
        compiler_params=_params("parallel", "arbitrary"),
        name="diff_attn_decode",
    )(page_table, q, k_new, v_new, lam_p, subln.reshape(1, DIFF_DV),
      *([cache_k] * DEC_PAGES), *([cache_v] * DEC_PAGES))


def _pad_lora(w_in, w_out):
    pad = LORA_PAD - w_in.shape[1]
    return (jnp.pad(w_in, ((0, 0), (0, pad))).astype(BF16), jnp.pad(w_out, ((0, pad), (0, 0))).astype(BF16))


def kernel(x_prompt, x_sample, mem_prompt, state_wkv, state_shift, cache_mem_k, cache_mem_v, cache_k, cache_v, page_table, ffn_norm, ffn_w13, ffn_w2, mix_norm, w_out, mem_norm, mem_w_kv, mem_q_norm, mem_k_norm, a_w_in, a_mu, a_w0, a_w1, a_w2, a_a0, a_a1, a_a2, a_g1, a_g2, a_k_k, a_k_a, a_r_k, a_lnx_w, a_lnx_b, kv_norm, kv_w, k_norm, b_w_in, b_q_norm, b_lam, b_subln):
    d = D_MODEL
    w13_b = ffn_w13.astype(BF16)
    w2_b = ffn_w2.astype(BF16)
    wout_b = w_out.astype(BF16)
    memw_b = mem_w_kv.astype(BF16)
    awin_b = a_w_in.astype(BF16)
    kvw_k, kvw_v = kv_w[:, :DIFF_W].astype(BF16), kv_w[:, DIFF_W:].astype(BF16)
    bq_b, bm_b = b_w_in[:, :, :DIFF_W].astype(BF16), b_w_in[:, :, DIFF_W:].astype(BF16)
    loras = []
    for i in range(N_A):
        w1, w2 = _pad_lora(a_w1[i], a_w2[i])
        a1, a2 = _pad_lora(a_a1[i], a_a2[i])
        loras.append(dict(w1=w1, w2=w2, a1=a1, a2=a2, g1=a_g1[i].astype(BF16), g2=a_g2[i].astype(BF16),
                          w0=a_w0[i].reshape(1, RWKV_W), a0=a_a0[i].reshape(1, RWKV_W),
                          k_k=a_k_k[i].reshape(1, RWKV_W), k_a=a_k_a[i].reshape(1, RWKV_W)))

    def run(x3, shift_prev, wkv0, mk, mv, pos, decode):
        bsz, t, _ = x3.shape
        m = bsz * t
        tm = min(512, m)
        x = x3.reshape(m, d)
        rope = rope_tables(jnp.broadcast_to(pos, (m,)) if decode else pos)
        rope_rows = m if decode else t
        shifts, states = [], []
        for i in range(N_A):
            x = ffn(x, ffn_norm[i, 0], w13_b[i, 0], w2_b[i, 0], tm)
            x_shift = jnp.concatenate([jnp.zeros((bsz, 1, d), F32), x.reshape(bsz, t, d)[:, :-1]], axis=1)
            xn, cat, mix = a_prep(x, x_shift.reshape(m, d), shift_prev[i], mix_norm[i], a_mu[i], tm, t)
            shifts.append(xn.reshape(bsz, t, d)[:, -1])
            proj = mm(cat, awin_b[i], tm)
            ld, kp, kn, bvec, gate = a_mix(proj, mix, loras[i], min(256, m))
            proj3 = proj.reshape(bsz, t, A_IN)
            if decode:
                heads = lambda z: z.reshape(bsz, RWKV_HEADS, 1, RWKV_N)
                y, s_new = wkv_step(
                    wkv0[i], heads(proj[:, :RWKV_W]), heads(ld), heads(kp),
                    proj[:, 2 * RWKV_W:3 * RWKV_W].reshape(bsz, RWKV_HEADS, RWKV_N, 1), heads(kn), heads(bvec))
                y = y.reshape(m, RWKV_W)
            else:
                r3 = lambda z: z.reshape(bsz, t, RWKV_W)
                y, s_pairs = wkv_scan(proj3, r3(ld), r3(kp), r3(kn), r3(bvec))
                y = y.reshape(m, RWKV_W)
                s_new = jnp.stack([s_pairs[:, :, :RWKV_N, :RWKV_N], s_pairs[:, :, RWKV_N:, RWKV_N:]], axis=2)
                s_new = jnp.swapaxes(s_new.reshape(bsz, RWKV_HEADS, RWKV_N, RWKV_N), -1, -2)
            states.append(s_new)
            y_mix = a_post(y, proj, kp, gate, a_lnx_w[i], a_lnx_b[i], a_r_k[i], min(256, m))
            o_mem = mem_attn(proj3, 3 * RWKV_W // MEM_W, mk[i], mv[i], mem_q_norm[i], min(512, t))
            x = out_mm(x, y_mix, o_mem.reshape(m, MEM_W), wout_b[i], tm)
            x = ffn(x, ffn_norm[i, 1], w13_b[i, 1], w2_b[i, 1], tm)
        k_rows = norm_mm(x, kv_norm, kvw_k, tm, 512, rope, k_norm, rope_rows)
        v_rows = norm_mm(x, kv_norm, kvw_v, tm, 512)
        k3 = k_rows.reshape(bsz, t, DIFF_W)
        v3 = v_rows.reshape(bsz, t, DIFF_W)
        for j in range(N_B):
            i = N_A + j
            lam_init = 0.8 - 0.6 * math.exp(-0.3 * i)
            x = ffn(x, ffn_norm[i, 0], w13_b[i, 0], w2_b[i, 0], tm)
            q = norm_mm(x, mix_norm[i], bq_b[j], tm, 512, rope, b_q_norm[j], rope_rows).reshape(bsz, t, DIFF_W)
            q_mem = norm_mm(x, mix_norm[i], bm_b[j], tm, 512).reshape(bsz, t, MEM_W)
            if decode:
                o = diff_attn_decode(q, k3, v3, cache_k.reshape(-1, PAGE_SIZE, DIFF_W),
                                     cache_v.reshape(-1, PAGE_SIZE, DIFF_W), page_table, b_lam[j], b_subln[j],
                                     lam_init)
            else:
                o = diff_attn_prompt(q, k3, v3, b_lam[j], b_subln[j], lam_init)
            o_mem = mem_attn(q_mem, 0, mk[i], mv[i], mem_q_norm[i], min(512, t))
            x = out_mm(x, o.reshape(m, DIFF_W), o_mem.reshape(m, MEM_W), wout_b[i], tm)
            x = ffn(x, ffn_norm[i, 1], w13_b[i, 1], w2_b[i, 1], tm)
        return x.reshape(bsz, t, d), jnp.stack(states), jnp.stack(shifts), k3, v3

    bp, tp, _ = x_prompt.shape
    bs, ts, _ = x_sample.shape
    assert ts == 1, "the sample group is decoded one token per sequence"
    mem2 = mem_prompt.reshape(bp * MEM_TOKENS, d)
    mk_p, mv_p = [], []
    for i in range(DEPTH):
        mk_i, mv_i = mem_kv(mem2, mem_norm[i], memw_b[i], mem_k_norm[i])
        mk_p.append(mk_i.reshape(bp, MEM_TOKENS, MEM_W))
        mv_p.append(mv_i.reshape(bp, MEM_TOKENS, MEM_W))

    y_p, wkv_p, shift_p, k_p, v_p = run(
        x_prompt, jnp.zeros((N_A, bp, d), F32), None, mk_p, mv_p, jnp.arange(tp), decode=False)
    mk_s = cache_mem_k.reshape(DEPTH, bs, MEM_TOKENS, MEM_W)
    mv_s = cache_mem_v.reshape(DEPTH, bs, MEM_TOKENS, MEM_W)
    y_s, wkv_s, shift_s, k_s, v_s = run(
        x_sample, state_shift, state_wkv, mk_s, mv_s, jnp.full((1,), PAST_LEN, jnp.int32), decode=True)

    kshape = lambda z, b_, t_: z.reshape(b_, t_, DIFF_HEADS, 2, DIFF_DH)
    vshape = lambda z, b_, t_: z.reshape(b_, t_, DIFF_HEADS, DIFF_DV)
    memshape = lambda zs: jnp.stack(zs).reshape(DEPTH, bp, MEM_TOKENS, MEM_HEADS, MEM_DH)
    return (y_p, y_s, wkv_p, shift_p, wkv_s, shift_s,
            kshape(k_p, bp, tp), vshape(v_p, bp, tp), kshape(k_s, bs, ts), vshape(v_s, bs, ts),
            memshape(mk_p), memshape(mv_p))
```

```python
import functools
import math

import jax
import jax.numpy as jnp
from jax import lax
from jax.experimental import pallas as pl
from jax.experimental.pallas import tpu as pltpu

F32 = jnp.float32
BF16 = jnp.bfloat16

D_MODEL = 2048
DEPTH = 4
N_A = 2
N_B = 2
MEM_TOKENS = 256
MEM_HEADS = 4
MEM_W = 512
MEM_DH = 128
RWKV_W = 1536
RWKV_N = 64
RWKV_HEADS = 24
LORA_PAD = 128
LORA_G = 256
A_IN = 3 * RWKV_W + MEM_W
DIFF_W = 1536
DIFF_DV = 128
DIFF_HEADS = 12
DIFF_DH = 64
ROT_DIM = 16
ROPE_THETA = 500000.0
D_FF = 5632
PAST_LEN = 16384
PAGE_SIZE = 128
NORM_EPS = 1e-6
LNX_EPS = 64e-5
SUBLN_EPS = 1e-5

LANES = 128
CHUNK = 64
VMEM_LIMIT = 56 * 1024 * 1024

NT_DIMS = (((1,), (1,)), ((), ()))
TN_DIMS = (((0,), (0,)), ((), ()))


def _params(*sem):
    return pltpu.CompilerParams(dimension_semantics=sem, vmem_limit_bytes=VMEM_LIMIT)


def _dot(a, b, dims=None):
    if dims is None:
        return jnp.dot(a, b, preferred_element_type=F32)
    return lax.dot_general(a, b, dims, preferred_element_type=F32)


def _rms(x, g, eps):
    return x * lax.rsqrt(jnp.mean(x * x, axis=-1, keepdims=True) + eps) * g


def _sigmoid(x):
    return 1.0 / (1.0 + jnp.exp(-x))


def _split2(x):
    hi = x.astype(BF16)
    lo = (x - hi.astype(F32)).astype(BF16)
    return hi, lo


def _group_matrix(scale):
    r = lax.broadcasted_iota(jnp.int32, (LANES, LANES), 0) // RWKV_N
    c = lax.broadcasted_iota(jnp.int32, (LANES, LANES), 1) // RWKV_N
    return jnp.where(r == c, scale, 0.0).astype(BF16)


def _group_sum(x, gm):
    hi, lo = _split2(x)
    return _dot(hi, gm) + _dot(lo, gm)


def _ffn_kernel(x_ref, g_ref, w1_ref, w3_ref, w2_ref, o_ref, h_ref, acc_ref):
    f = pl.program_id(1)

    @pl.when(f == 0)
    def _():
        h_ref[...] = _rms(x_ref[...], g_ref[...], NORM_EPS).astype(BF16)
        acc_ref[...] = jnp.zeros_like(acc_ref)

    h = h_ref[...]
    gate = _dot(h, w1_ref[...])
    up = _dot(h, w3_ref[...])
    act = (gate * _sigmoid(gate) * up).astype(BF16)
    acc_ref[...] += _dot(act, w2_ref[...])

    @pl.when(f == pl.num_programs(1) - 1)
    def _():
        o_ref[...] = x_ref[...] + 0.5 * acc_ref[...]


def ffn(x, g, w13, w2, layer, half, tm, tf=512):
    m, d = x.shape
    nf = D_FF // tf
    return pl.pallas_call(
        _ffn_kernel,
        grid=(m // tm, nf),
        in_specs=[
            pl.BlockSpec((tm, d), lambda i, f: (i, 0)),
            pl.BlockSpec((1, d), lambda i, f: (0, 0)),
            pl.BlockSpec((None, None, d, tf), lambda i, f: (layer, half, 0, f)),
            pl.BlockSpec((None, None, d, tf), lambda i, f: (layer, half, 0, f + nf)),
            pl.BlockSpec((None, None, tf, d), lambda i, f: (layer, half, f, 0)),
        ],
        out_specs=pl.BlockSpec((tm, d), lambda i, f: (i, 0)),
        out_shape=jax.ShapeDtypeStruct((m, d), F32),
        scratch_shapes=[pltpu.VMEM((tm, d), BF16), pltpu.VMEM((tm, d), F32)],
        compiler_params=_params("parallel", "arbitrary"),
        name="ffn",
    )(x, g.reshape(1, d), w13, w13, w2)


def _norm_mm_kernel(*refs, qk_epilogue, mxu_copy_scale):
    refs = list(refs)
    h_ref = refs.pop()
    ob_ref = refs.pop() if mxu_copy_scale is not None else None
    o_ref = refs.pop()
    x_ref, g_ref, w_ref = refs[:3]

    @pl.when(pl.program_id(1) == 0)
    def _():
        h_ref[...] = _rms(x_ref[...], g_ref[...], NORM_EPS).astype(BF16)

    def emit(cols, val):
        o_ref[:, cols] = val
        if ob_ref is not None:
            ob_ref[:, cols] = (val * mxu_copy_scale).astype(BF16)

    y = _dot(h_ref[...], w_ref[...])
    if not qk_epilogue:
        emit(slice(None), y)
        return
    gh_ref, cos_ref, s1_ref, s2_ref = refs[3:7]
    gm = _group_matrix(1.0 / DIFF_DH)
    gh, cos, s1, s2 = gh_ref[...], cos_ref[...], s1_ref[...], s2_ref[...]
    half = ROT_DIM // 2
    for j in range(y.shape[1] // LANES):
        blk = y[:, j * LANES:(j + 1) * LANES]
        nb = blk * lax.rsqrt(_group_sum(blk * blk, gm) + NORM_EPS) * gh
        emit(slice(j * LANES, (j + 1) * LANES),
             nb * cos + pltpu.roll(nb, LANES - half, 1) * s1 + pltpu.roll(nb, half, 1) * s2)


def norm_mm(x, g, w, tm, tn, rope=None, head_gain=None, rows_per_seq=None, mxu_copy_scale=None):
    m, d = x.shape
    n = w.shape[1]
    qk = rope is not None
    in_specs = [
        pl.BlockSpec((tm, d), lambda i, j: (i, 0)),
        pl.BlockSpec((1, d), lambda i, j: (0, 0)),
        pl.BlockSpec((d, tn), lambda i, j: (0, j)),
    ]
    args = [x, g.reshape(1, d), w]
    if qk:
        tiles_per_seq = rows_per_seq // tm
        in_specs.append(pl.BlockSpec((1, LANES), lambda i, j: (0, 0)))
        args.append(jnp.tile(head_gain.reshape(1, DIFF_DH), (1, 2)))
        for t in rope:
            in_specs.append(pl.BlockSpec((tm, LANES), lambda i, j: (i % tiles_per_seq, 0)))
            args.append(t)
    out_spec = pl.BlockSpec((tm, tn), lambda i, j: (i, j))
    out_specs, out_shape = [out_spec], [jax.ShapeDtypeStruct((m, n), F32)]
    if mxu_copy_scale is not None:
        out_specs.append(out_spec)
        out_shape.append(jax.ShapeDtypeStruct((m, n), BF16))
    res = pl.pallas_call(
        functools.partial(_norm_mm_kernel, qk_epilogue=qk, mxu_copy_scale=mxu_copy_scale),
        grid=(m // tm, n // tn),
        in_specs=in_specs,
        out_specs=out_specs,
        out_shape=out_shape,
        scratch_shapes=[pltpu.VMEM((tm, d), BF16)],
        compiler_params=_params("parallel", "arbitrary"),
        name="norm_mm_qk" if qk else "norm_mm",
    )(*args)
    return res if mxu_copy_scale is not None else res[0]


def rope_tables(pos):
    half = ROT_DIM // 2
    inv = ROPE_THETA ** (-jnp.arange(0, ROT_DIM, 2, dtype=F32) / ROT_DIM)
    ang = pos.astype(F32)[:, None] * inv[None, :]
    cos, sin = jnp.cos(ang), jnp.sin(ang)
    t = pos.shape[0]
    rest = DIFF_DH - ROT_DIM
    c64 = jnp.concatenate([cos, cos, jnp.ones((t, rest), F32)], axis=1)
    s1_64 = jnp.concatenate([-sin, jnp.zeros((t, DIFF_DH - half), F32)], axis=1)
    s2_64 = jnp.concatenate([jnp.zeros((t, half), F32), sin, jnp.zeros((t, rest), F32)], axis=1)
    return tuple(jnp.tile(z, (1, 2)) for z in (c64, s1_64, s2_64))


def _mem_kv_kernel(x_ref, g_ref, w_ref, gk_ref, k_ref, v_ref):
    h = _rms(x_ref[...], g_ref[...], NORM_EPS).astype(BF16)
    y = _dot(h, w_ref[...])
    gk = gk_ref[...]
    for j in range(MEM_HEADS):
        blk = y[:, j * MEM_DH:(j + 1) * MEM_DH]
        k_ref[:, j * MEM_DH:(j + 1) * MEM_DH] = _rms(blk, gk, NORM_EPS)
    v_ref[...] = y[:, MEM_W:]


def mem_kv(mem, g, w, gk):
    m, d = mem.shape
    return pl.pallas_call(
        _mem_kv_kernel,
        grid=(1,),
        in_specs=[
            pl.BlockSpec((m, d), lambda i: (0, 0)),
            pl.BlockSpec((1, d), lambda i: (0, 0)),
            pl.BlockSpec((d, 2 * MEM_W), lambda i: (0, 0)),
            pl.BlockSpec((1, MEM_DH), lambda i: (0, 0)),
        ],
        out_specs=[pl.BlockSpec((m, MEM_W), lambda i: (0, 0))] * 2,
        out_shape=[jax.ShapeDtypeStruct((m, MEM_W), F32)] * 2,
        compiler_params=_params("arbitrary"),
        name="mem_kv",
    )(mem, g.reshape(1, d), w, gk.reshape(1, MEM_DH))


def _mem_attn_kernel(q_ref, k_ref, v_ref, gq_ref, o_ref):
    q = q_ref[0]
    rows = q.shape[0]
    if rows < 8:
        q = jnp.broadcast_to(q, (8, q.shape[1]))
    k = k_ref[0]
    v = v_ref[0]
    gq = gq_ref[...]
    for h in range(MEM_HEADS):
        sl = slice(h * MEM_DH, (h + 1) * MEM_DH)
        qh = _rms(q[:, sl], gq, NORM_EPS).astype(BF16)
        s = _dot(qh, k[:, sl].astype(BF16), NT_DIMS) * (MEM_DH ** -0.5)
        p = jnp.exp(s - jnp.max(s, axis=-1, keepdims=True))
        o = _dot(p.astype(BF16), v[:, sl].astype(BF16)) / jnp.sum(p, axis=-1, keepdims=True)
        o_ref[0, :, sl] = o[:rows].astype(o_ref.dtype)


def mem_attn(proj, q_col_block, mk, mv, gq, tq):
    b, t, _ = proj.shape
    return pl.pallas_call(
        _mem_attn_kernel,
        grid=(b, t // tq),
        in_specs=[
            pl.BlockSpec((1, tq, MEM_W), lambda i, j: (i, j, q_col_block)),
            pl.BlockSpec((1, MEM_TOKENS, MEM_W), lambda i, j: (i, 0, 0)),
            pl.BlockSpec((1, MEM_TOKENS, MEM_W), lambda i, j: (i, 0, 0)),
            pl.BlockSpec((1, MEM_DH), lambda i, j: (0, 0)),
        ],
        out_specs=pl.BlockSpec((1, tq, MEM_W), lambda i, j: (i, j, 0)),
        out_shape=jax.ShapeDtypeStruct((b, t, MEM_W), BF16),
        compiler_params=_params("parallel", "arbitrary"),
        name="mem_attn",
    )(proj, mk, mv, gq.reshape(1, MEM_DH))


def _out_mm_kernel(x_ref, a_ref, b_ref, wa_ref, wb_ref, o_ref):
    o_ref[...] = x_ref[...] + _dot(a_ref[...], wa_ref[...]) + _dot(b_ref[...], wb_ref[...])


def out_mm(x, a, b, w, layer, tm, tn=512):
    m, d = x.shape
    ka, kb = a.shape[1], b.shape[1]
    kb_blocks = ka // kb
    return pl.pallas_call(
        _out_mm_kernel,
        grid=(m // tm, d // tn),
        in_specs=[
            pl.BlockSpec((tm, tn), lambda i, j: (i, j)),
            pl.BlockSpec((tm, ka), lambda i, j: (i, 0)),
            pl.BlockSpec((tm, kb), lambda i, j: (i, 0)),
            pl.BlockSpec((None, ka, tn), lambda i, j: (layer, 0, j)),
            pl.BlockSpec((None, kb, tn), lambda i, j: (layer, kb_blocks, j)),
        ],
        out_specs=pl.BlockSpec((tm, tn), lambda i, j: (i, j)),
        out_shape=jax.ShapeDtypeStruct((m, d), F32),
        compiler_params=_params("parallel", "arbitrary"),
        name="out_mm",
    )(x, a, b, w, w)


def _a_prep_kernel(x_ref, xs_ref, sp_ref, g_ref, mu_ref, xn_ref, cat_ref, mix_ref, *, tiles_per_seq):
    g = g_ref[...]
    xn = _rms(x_ref[...], g, NORM_EPS)
    if tiles_per_seq is None:
        xp = sp_ref[...]
    else:
        xp = _rms(xs_ref[...], g, NORM_EPS)
        first = pl.program_id(0) % tiles_per_seq == 0
        row = lax.broadcasted_iota(jnp.int32, xn.shape, 0)
        xp = jnp.where(jnp.logical_and(row == 0, first), sp_ref[0], xp)
    xx = xp - xn
    d = xn.shape[1]
    xn_ref[...] = xn
    cat_ref[:, :d] = xn.astype(BF16)
    cat_ref[:, d:] = xx.astype(BF16)
    for i in range(3):
        mix_ref[:, i * d:(i + 1) * d] = (xn + xx * mu_ref[i:i + 1, :]).astype(BF16)


def a_prep(x, x_shift, shift_prev, g, mu, tm, rows_per_seq):
    m, d = x.shape
    if rows_per_seq == 1:
        tiles_per_seq = None
        x_shift = x
        sp, sp_spec = shift_prev, pl.BlockSpec((tm, d), lambda i: (i, 0))
    else:
        tiles_per_seq = rows_per_seq // tm
        sp = shift_prev.reshape(shift_prev.shape[0], 1, d)
        sp_spec = pl.BlockSpec((1, 1, d), lambda i: (i // tiles_per_seq, 0, 0))
    return pl.pallas_call(
        functools.partial(_a_prep_kernel, tiles_per_seq=tiles_per_seq),
        grid=(m // tm,),
        in_specs=[
            pl.BlockSpec((tm, d), lambda i: (i, 0)),
            pl.BlockSpec((tm, d), lambda i: (i, 0)),
            sp_spec,
            pl.BlockSpec((1, d), lambda i: (0, 0)),
            pl.BlockSpec((3, d), lambda i: (0, 0)),
        ],
        out_specs=[
            pl.BlockSpec((tm, d), lambda i: (i, 0)),
            pl.BlockSpec((tm, 2 * d), lambda i: (i, 0)),
            pl.BlockSpec((tm, 3 * d), lambda i: (i, 0)),
        ],
        out_shape=[
            jax.ShapeDtypeStruct((m, d), F32),
            jax.ShapeDtypeStruct((m, 2 * d), BF16),
            jax.ShapeDtypeStruct((m, 3 * d), BF16),
        ],
        compiler_params=_params("parallel"),
        name="a_prep",
    )(x, x_shift, sp, g.reshape(1, d), mu)


def _mm_kernel(a_ref, w_ref, o_ref):
    o_ref[...] = _dot(a_ref[...], w_ref[...])


def mm(a, w, layer, tm, tn=512):
    m, k = a.shape
    n = w.shape[2]
    return pl.pallas_call(
        _mm_kernel,
        grid=(m // tm, n // tn),
        in_specs=[pl.BlockSpec((tm, k), lambda i, j: (i, 0)),
                  pl.BlockSpec((None, k, tn), lambda i, j: (layer, 0, j))],
        out_specs=pl.BlockSpec((tm, tn), lambda i, j: (i, j)),
        out_shape=jax.ShapeDtypeStruct((m, n), F32),
        compiler_params=_params("parallel", "arbitrary"),
        name="mm",
    )(a, w)


def _a_mix_kernel(k_ref, mix_ref, w1_ref, w2_ref, a1_ref, a2_ref, g1_ref, g2_ref, w0_ref, a0_ref, kk_ref,
                  ka_ref, ld_ref, kp_ref, kn_ref, b_ref, g_ref):
    d = D_MODEL
    xw, xa, xg = mix_ref[:, :d], mix_ref[:, d:2 * d], mix_ref[:, 2 * d:]
    wl = _dot(jnp.tanh(_dot(xw, w1_ref[...])).astype(BF16), w2_ref[...]) + w0_ref[...]
    w = -(jnp.maximum(-wl, 0.0) + jnp.log(1.0 + jnp.exp(-jnp.abs(wl)))) - 0.5
    ld_ref[...] = -jnp.exp(w)
    a = _sigmoid(_dot(_dot(xa, a1_ref[...]).astype(BF16), a2_ref[...]) + a0_ref[...])
    g_ref[...] = _dot(_sigmoid(_dot(xg, g1_ref[...])).astype(BF16), g2_ref[...])
    k = k_ref[...]
    kp_ref[...] = k * (1.0 + (a - 1.0) * ka_ref[...])
    kraw = k * kk_ref[...]
    gm = _group_matrix(1.0)
    for j in range(RWKV_W // LANES):
        sl = slice(j * LANES, (j + 1) * LANES)
        blk = kraw[:, sl]
        kn = blk / jnp.maximum(jnp.sqrt(_group_sum(blk * blk, gm)), 1e-12)
        kn_ref[:, sl] = kn
        b_ref[:, sl] = kn * a[:, sl]


def a_mix(proj, mix, lw, tm):
    m = proj.shape[0]
    row = lambda i: (i, 0)
    fixed = lambda i: (0, 0)
    vec = pl.BlockSpec((1, RWKV_W), fixed)
    out = jax.ShapeDtypeStruct((m, RWKV_W), F32)
    return pl.pallas_call(
        _a_mix_kernel,
        grid=(m // tm,),
        in_specs=[
            pl.BlockSpec((tm, RWKV_W), lambda i: (i, 1)),
            pl.BlockSpec((tm, 3 * D_MODEL), row),
            pl.BlockSpec((D_MODEL, LORA_PAD), fixed), pl.BlockSpec((LORA_PAD, RWKV_W), fixed),
            pl.BlockSpec((D_MODEL, LORA_PAD), fixed), pl.BlockSpec((LORA_PAD, RWKV_W), fixed),
            pl.BlockSpec((D_MODEL, LORA_G), fixed), pl.BlockSpec((LORA_G, RWKV_W), fixed),
            vec, vec, vec, vec,
        ],
        out_specs=[pl.BlockSpec((tm, RWKV_W), row)] * 5,
        out_shape=[out] * 5,
        compiler_params=_params("parallel"),
        name="a_mix",
    )(proj, mix, lw["w1"], lw["w2"], lw["a1"], lw["a2"], lw["g1"], lw["g2"],
      lw["w0"], lw["a0"], lw["k_k"], lw["k_a"])


def _wkv_masks(c):
    c2 = 2 * c
    ri = lax.broadcasted_iota(jnp.int32, (c2, c2), 0)
    ci = lax.broadcasted_iota(jnp.int32, (c2, c2), 1)
    same = (ri // c) == (ci // c)
    masks = dict(
        in_head0=lax.broadcasted_iota(jnp.int32, (c, LANES), 1) < RWKV_N,
        tri=jnp.where(lax.broadcasted_iota(jnp.int32, (c, c), 0) >= lax.broadcasted_iota(jnp.int32, (c, c), 1),
                      1.0, 0.0).astype(BF16),
        strict=jnp.logical_and(same, (ri % c) > (ci % c)),
        incl=jnp.logical_and(same, (ri % c) >= (ci % c)),
        eye=jnp.where(ri == ci, 1.0, 0.0),
        diag8=(ri // 8) == (ci // 8),
        lower_left=[],
    )
    size = 8
    while size < c:
        masks["lower_left"].append(jnp.logical_and(
            (ri // (2 * size)) == (ci // (2 * size)),
            jnp.logical_and((ri // size) % 2 == 1, (ci // size) % 2 == 0)))
        size *= 2
    return masks


def _each(f, *lists):
    return [f(*args) for args in zip(*lists)]


def _wkv_chunk(s, r, ld, k, v, kn, b, mk):
    c = r[0].shape[0]
    c2 = 2 * c
    in_head0 = mk["in_head0"]
    tri = mk["tri"]

    def stack(z):
        return jnp.concatenate([jnp.where(in_head0, z, 0.0), jnp.where(in_head0, 0.0, z)], axis=0)

    def bdot(x, y):
        return _dot(x.astype(BF16), y.astype(BF16))

    p1 = _each(lambda z: z.astype(BF16), ld)
    rem = _each(lambda z, p: z - p.astype(F32), ld, p1)
    p2 = _each(lambda z: z.astype(BF16), rem)
    p3 = _each(lambda z, p: (z - p.astype(F32)).astype(BF16), rem, p2)
    cum = _each(lambda a1, a2, a3: _dot(tri, a1) + _dot(tri, a2) + _dot(tri, a3), p1, p2, p3)
    eg = _each(jnp.exp, cum)
    einv = _each(lambda z: jnp.exp(-z), cum)
    at_s = _each(lambda n, z, d: stack(-n * jnp.exp(z - d)).astype(BF16), kn, cum, ld)
    rt_s = _each(lambda x, e: stack(x * e).astype(BF16), r, eg)
    bt = _each(lambda x, e: x * e, b, einv)
    kt = _each(lambda x, e: x * e, k, einv)
    v_s = _each(lambda x: stack(x).astype(BF16), v)

    gmat = _each(lambda a, x, y, z: _dot(jnp.concatenate([a, x], axis=0),
                                         jnp.concatenate([y, y, z, z], axis=0).astype(BF16), NT_DIMS),
                 at_s, rt_s, bt, kt)
    a_ab = _each(lambda g: jnp.where(mk["strict"], g[:c2, :c2], 0.0), gmat)
    a_ak = _each(lambda g: jnp.where(mk["strict"], g[:c2, c2:], 0.0).astype(BF16), gmat)
    a_rr = _each(lambda g: jnp.concatenate([jnp.where(mk["incl"], g[c2:, :c2], 0.0),
                                            jnp.where(mk["incl"], g[c2:, c2:], 0.0)], axis=1).astype(BF16), gmat)

    ad = _each(lambda a: jnp.where(mk["diag8"], a, 0.0), a_ab)
    ad2 = _each(bdot, ad, ad)
    ad4 = _each(bdot, ad2, ad2)
    inv = _each(lambda a: mk["eye"] + a, ad)
    inv = _each(lambda x, y: x + bdot(x, y), inv, ad2)
    inv = _each(lambda x, y: x + bdot(x, y), inv, ad4)
    for lower_left in mk["lower_left"]:
        half = _each(lambda x, a: bdot(x, jnp.where(lower_left, a, 0.0)), inv, a_ab)
        inv = _each(lambda x, y: x + bdot(y, x), inv, half)

    s_b = _each(lambda z: z.astype(BF16), s)
    rhs = _each(lambda a, x, y, z: _dot(a, x) + _dot(y, z, NT_DIMS), a_ak, v_s, at_s, s_b)
    u = _each(lambda x, y: bdot(x, y).astype(BF16), inv, rhs)
    uv = _each(lambda x, y: jnp.concatenate([x, y], axis=0), u, v_s)
    y_s = _each(lambda x, z, a, w: _dot(x, z, NT_DIMS) + _dot(a, w), rt_s, s_b, a_rr, uv)
    ds = _each(lambda w, x, y: _dot(w, jnp.concatenate([stack(x), stack(y)], axis=0).astype(BF16), TN_DIMS),
               uv, bt, kt)
    y = _each(lambda z: z[:c] + z[c:], y_s)
    s_new = _each(lambda z, dz, e: (z + dz) * e[c - 1:c, :], s, ds, eg)
    return y, s_new


def _wkv_kernel(r_ref, ld_ref, k_ref, v_ref, kn_ref, b_ref, y_ref, s_ref, st_ref, *, chunks, pairs):
    @pl.when(pl.program_id(2) == 0)
    def _():
        st_ref[...] = jnp.zeros_like(st_ref)

    mk = _wkv_masks(CHUNK)
    s = [st_ref[p] for p in range(pairs)]
    for i in range(chunks):
        rows = slice(i * CHUNK, (i + 1) * CHUNK)
        cut = lambda ref: [ref[0, rows, p * LANES:(p + 1) * LANES] for p in range(pairs)]
        y, s = _wkv_chunk(s, cut(r_ref), cut(ld_ref), cut(k_ref), cut(v_ref), cut(kn_ref), cut(b_ref), mk)
        for p in range(pairs):
            y_ref[0, rows, p * LANES:(p + 1) * LANES] = y[p]
    for p in range(pairs):
        st_ref[p] = s[p]

    @pl.when(pl.program_id(2) == pl.num_programs(2) - 1)
    def _():
        s_ref[0] = st_ref[...]


WKV_PAIRS = 12
WKV_CHUNKS = 2


def wkv_scan(proj, ld, kp, kn, b):
    bsz, t, _ = proj.shape
    groups = RWKV_W // LANES // WKV_PAIRS
    tb = WKV_CHUNKS * CHUNK
    width = WKV_PAIRS * LANES
    blk = lambda off: pl.BlockSpec((1, tb, width), lambda i, p, c: (i, c, p + off))
    return pl.pallas_call(
        functools.partial(_wkv_kernel, chunks=WKV_CHUNKS, pairs=WKV_PAIRS),
        grid=(bsz, groups, t // tb),
        in_specs=[blk(0), blk(0), blk(0), blk(2 * groups), blk(0), blk(0)],
        out_specs=[
            pl.BlockSpec((1, tb, width), lambda i, p, c: (i, c, p)),
            pl.BlockSpec((1, WKV_PAIRS, LANES, LANES), lambda i, p, c: (i, p, 0, 0)),
        ],
        out_shape=[
            jax.ShapeDtypeStruct((bsz, t, RWKV_W), F32),
            jax.ShapeDtypeStruct((bsz, RWKV_W // LANES, LANES, LANES), F32),
        ],
        scratch_shapes=[pltpu.VMEM((WKV_PAIRS, LANES, LANES), F32)],
        compiler_params=_params("parallel", "parallel", "arbitrary"),
        name="wkv_scan",
    )(proj, ld, kp, proj, kn, b)


def _wkv_step_kernel(s_ref, r_ref, ld_ref, k_ref, v_ref, kn_ref, b_ref, y_ref, so_ref):
    s = s_ref[0]
    sa = jnp.sum(s * (-kn_ref[0]), axis=-1, keepdims=True)
    s = s * jnp.exp(ld_ref[0]) + sa * b_ref[0] + v_ref[0] * k_ref[0]
    so_ref[0] = s
    y_ref[0] = jnp.sum(s * r_ref[0], axis=-1, keepdims=True)


def wkv_step(s0, r, ld, k, v, kn, b):
    bsz = s0.shape[0]
    h, n = RWKV_HEADS, RWKV_N
    st = pl.BlockSpec((1, h, n, n), lambda i: (i, 0, 0, 0))
    rw = pl.BlockSpec((1, h, 1, n), lambda i: (i, 0, 0, 0))
    cl = pl.BlockSpec((1, h, n, 1), lambda i: (i, 0, 0, 0))
    return pl.pallas_call(
        _wkv_step_kernel,
        grid=(bsz,),
        in_specs=[st, rw, rw, rw, cl, rw, rw],
        out_specs=[cl, st],
        out_shape=[jax.ShapeDtypeStruct((bsz, h, n, 1), F32), jax.ShapeDtypeStruct((bsz, h, n, n), F32)],
        compiler_params=_params("parallel"),
        name="wkv_step",
    )(s0, r, ld, k, v, kn, b)


def _a_post_kernel(y_ref, r_ref, kp_ref, v_ref, g_ref, lw_ref, lb_ref, rk_ref, o_ref):
    gsum = _group_matrix(1.0)
    gmean = _group_matrix(1.0 / RWKV_N)
    for j in range(RWKV_W // LANES):
        sl = slice(j * LANES, (j + 1) * LANES)
        y = y_ref[:, sl]
        cen = y - _group_sum(y, gmean)
        yn = cen * lax.rsqrt(_group_sum(cen * cen, gmean) + LNX_EPS) * lw_ref[:, sl] + lb_ref[:, sl]
        bonus = _group_sum(r_ref[:, sl] * kp_ref[:, sl] * rk_ref[:, sl], gsum) * v_ref[:, sl]
        o_ref[:, sl] = ((yn + bonus) * g_ref[:, sl]).astype(o_ref.dtype)


def a_post(y, proj, kp, g, lnx_w, lnx_b, r_k, tm):
    m = y.shape[0]
    row = lambda i: (i, 0)
    vec = pl.BlockSpec((1, RWKV_W), lambda i: (0, 0))
    tile = pl.BlockSpec((tm, RWKV_W), row)
    return pl.pallas_call(
        _a_post_kernel,
        grid=(m // tm,),
        in_specs=[tile, pl.BlockSpec((tm, RWKV_W), lambda i: (i, 0)), tile,
                  pl.BlockSpec((tm, RWKV_W), lambda i: (i, 2)), tile, vec, vec, vec],
        out_specs=tile,
        out_shape=jax.ShapeDtypeStruct((m, RWKV_W), BF16),
        compiler_params=_params("parallel"),
        name="a_post",
    )(y, proj, kp, proj, g, lnx_w.reshape(1, RWKV_W), lnx_b.reshape(1, RWKV_W), r_k.reshape(1, RWKV_W))


def _lambda(lam_ref, lam_init):
    lq = lam_ref[...]
    l1 = jnp.sum(lq[0:1] * lq[1:2], axis=-1, keepdims=True)
    l2 = jnp.sum(lq[2:3] * lq[3:4], axis=-1, keepdims=True)
    return jnp.exp(l1) - jnp.exp(l2) + lam_init


ATTN_HEADS = 2
ATTN_TQ = 512
LOG2E = 1.4426950408889634
Q_SCALE = DIFF_DH ** -0.5 * LOG2E


def _diff_attn_kernel(q_ref, k_ref, v_ref, lam_ref, gs_ref, o_ref, m_ref, l_ref, acc_ref, *, tq, lam_init):
    qi = pl.program_id(2)
    ki = pl.program_id(3)

    @pl.when(ki == 0)
    def _():
        m_ref[...] = jnp.full_like(m_ref, -jnp.inf)
        l_ref[...] = jnp.zeros_like(l_ref)
        acc_ref[...] = jnp.zeros_like(acc_ref)

    def step(diagonal):
        lane = lax.broadcasted_iota(jnp.int32, (tq, DIFF_DV), 1)
        if diagonal:
            visible = (lax.broadcasted_iota(jnp.int32, (tq, tq), 1)
                       <= lax.broadcasted_iota(jnp.int32, (tq, tq), 0))
        for h in range(ATTN_HEADS):
            cols = slice(h * DIFF_DV, (h + 1) * DIFF_DV)
            q = q_ref[0, :, cols]
            kb = k_ref[0, :, cols]
            vb = v_ref[0, :, cols]
            for c in range(2):
                idx = 2 * h + c
                in_comp = (lane < DIFF_DH) if c == 0 else (lane >= DIFF_DH)
                s = _dot(jnp.where(in_comp, q, jnp.zeros_like(q)), kb, NT_DIMS)
                if diagonal:
                    s = jnp.where(visible, s, -jnp.inf)
                m_old = m_ref[idx]
                m_new = jnp.maximum(m_old, jnp.max(s, axis=-1, keepdims=True))
                alpha = jnp.exp2(m_old - m_new)
                p = jnp.exp2(s - jnp.concatenate([m_new] * (tq // LANES), axis=1))
                l_ref[idx] = alpha * l_ref[idx] + jnp.sum(p, axis=-1, keepdims=True)
                acc_ref[idx] = alpha * acc_ref[idx] + _dot(p.astype(BF16), vb)
                m_ref[idx] = m_new

    @pl.when(ki < qi)
    def _():
        step(False)

    @pl.when(ki == qi)
    def _():
        step(True)

    @pl.when(ki == pl.num_programs(3) - 1)
    def _():
        lam = _lambda(lam_ref, lam_init)
        for h in range(ATTN_HEADS):
            o = acc_ref[2 * h] / l_ref[2 * h] - lam * (acc_ref[2 * h + 1] / l_ref[2 * h + 1])
            o_ref[0, :, h * DIFF_DV:(h + 1) * DIFF_DV] = (
                _rms(o, gs_ref[...], SUBLN_EPS) * (1.0 - lam_init)).astype(o_ref.dtype)


def diff_attn_prompt(q, k, v, lam_p, subln, lam_init):
    b, t, _ = q.shape
    tq = ATTN_TQ
    width = ATTN_HEADS * DIFF_DV
    chains = 2 * ATTN_HEADS
    kv_map = lambda i, h, qi, ki: (i, jnp.minimum(ki, qi), h)
    return pl.pallas_call(
        functools.partial(_diff_attn_kernel, tq=tq, lam_init=lam_init),
        grid=(b, DIFF_HEADS // ATTN_HEADS, t // tq, t // tq),
        in_specs=[
            pl.BlockSpec((1, tq, width), lambda i, h, qi, ki: (i, qi, h)),
            pl.BlockSpec((1, tq, width), kv_map),
            pl.BlockSpec((1, tq, width), kv_map),
            pl.BlockSpec((4, DIFF_DH), lambda i, h, qi, ki: (0, 0)),
            pl.BlockSpec((1, DIFF_DV), lambda i, h, qi, ki: (0, 0)),
        ],
        out_specs=pl.BlockSpec((1, tq, width), lambda i, h, qi, ki: (i, qi, h)),
        out_shape=jax.ShapeDtypeStruct((b, t, DIFF_W), BF16),
        scratch_shapes=[pltpu.VMEM((chains, tq, LANES), F32), pltpu.VMEM((chains, tq, LANES), F32),
                        pltpu.VMEM((chains, tq, DIFF_DV), F32)],
        compiler_params=_params("parallel", "parallel", "parallel", "arbitrary"),
        name="diff_attn_prompt",
    )(q, k, v, lam_p, subln.reshape(1, DIFF_DV))


DEC_PAGES = 4
DEC_ROWS = 8


def _dec_attn_kernel(pt_ref, q_ref, kn_ref, vn_ref, lam_ref, gs_ref, *refs, lam_init):
    k_refs = refs[:DEC_PAGES]
    v_refs = refs[DEC_PAGES:2 * DEC_PAGES]
    o_ref, qb_ref, m_ref, l_ref, acc_ref = refs[2 * DEC_PAGES:]
    step = pl.program_id(1)
    row = lax.broadcasted_iota(jnp.int32, (DIFF_HEADS, DEC_ROWS, LANES), 1)

    def comp_rows(prod):
        s0 = jnp.sum(prod[:, :DIFF_DH, :], axis=1, keepdims=True)
        s1 = jnp.sum(prod[:, DIFF_DH:, :], axis=1, keepdims=True)
        return jnp.where(row == 0, s0, jnp.where(row == 1, s1, 0.0))

    @pl.when(step == 0)
    def _():
        q_col = q_ref[0] * (DIFF_DH ** -0.5)
        qb_ref[...] = jnp.broadcast_to(q_col, qb_ref.shape)
        m_ref[...] = comp_rows(q_col * kn_ref[0])
        l_ref[...] = jnp.ones_like(l_ref)
        acc_ref[...] = jnp.broadcast_to(vn_ref[0], acc_ref.shape)

    qb = qb_ref[...]
    for k_ref, v_ref in zip(k_refs, v_refs):
        s = comp_rows(k_ref[0] * qb)
        m_old = m_ref[...]
        m_new = jnp.maximum(m_old, jnp.max(s, axis=-1, keepdims=True))
        alpha = jnp.exp(m_old - m_new)
        p = jnp.exp(s - m_new)
        l_ref[...] = alpha * l_ref[...] + jnp.sum(p, axis=-1, keepdims=True)
        pv = lax.dot_general(p.astype(BF16), v_ref[0].astype(BF16), (((2,), (1,)), ((0,), (0,))),
                             preferred_element_type=F32)
        acc_ref[...] = alpha * acc_ref[...] + pv
        m_ref[...] = m_new

    @pl.when(step == pl.num_programs(1) - 1)
    def _():
        lam = _lambda(lam_ref, lam_init)
        w = acc_ref[...] / l_ref[...]
        o = w[:, 0:1, :] - lam * w[:, 1:2, :]
        o = o * lax.rsqrt(jnp.mean(o * o, axis=-1, keepdims=True) + SUBLN_EPS) * gs_ref[...] * (1.0 - lam_init)
        o_ref[0] = jnp.broadcast_to(o, o_ref.shape[1:])


def diff_attn_decode(q_col, k_col, v_new, cache_kt, cache_v, page_table, lam_p, subln, lam_init):
    b = q_col.shape[0]
    n_pages = page_table.shape[1]
    col = pl.BlockSpec((1, DIFF_HEADS, DIFF_DV, 1), lambda i, s, pt: (i, 0, 0, 0))
    state = pltpu.VMEM((DIFF_HEADS, DEC_ROWS, LANES), F32)

    def page_spec(j):
        return pl.BlockSpec((1, DIFF_HEADS, PAGE_SIZE, DIFF_DV), lambda i, s, pt: (pt[i, s * DEC_PAGES + j], 0, 0, 0))

    grid_spec = pltpu.PrefetchScalarGridSpec(
        num_scalar_prefetch=1,
        grid=(b, n_pages // DEC_PAGES),
        in_specs=[col, col,
                  pl.BlockSpec((1, DIFF_HEADS, 1, DIFF_DV), lambda i, s, pt: (i, 0, 0, 0)),
                  pl.BlockSpec((4, DIFF_DH), lambda i, s, pt: (0, 0)),
                  pl.BlockSpec((1, DIFF_DV), lambda i, s, pt: (0, 0))]
        + [page_spec(j) for j in range(DEC_PAGES)] * 2,
        out_specs=pl.BlockSpec((1, DIFF_HEADS, DEC_ROWS, DIFF_DV), lambda i, s, pt: (i, 0, 0, 0)),
        scratch_shapes=[pltpu.VMEM((DIFF_HEADS, DIFF_DV, PAGE_SIZE), F32), state, state, state],
    )
    return pl.pallas_call(
        functools.partial(_dec_attn_kernel, lam_init=lam_init),
        grid_spec=grid_spec,
        out_shape=jax.ShapeDtypeStruct((b, DIFF_HEADS, DEC_ROWS, DIFF_DV), F32),
        compiler_params=_params("parallel", "arbitrary"),
        name="diff_attn_decode",
    )(page_table, q_col, k_col, v_new, lam_p, subln.reshape(1, DIFF_DV),
      *([cache_kt] * DEC_PAGES), *([cache_v] * DEC_PAGES))


def _pad_lora(w_in, w_out):
    pad = LORA_PAD - w_in.shape[1]
    return (jnp.pad(w_in, ((0, 0), (0, pad))).astype(BF16), jnp.pad(w_out, ((0, pad), (0, 0))).astype(BF16))


def kernel(x_prompt, x_sample, mem_prompt, state_wkv, state_shift, cache_mem_k, cache_mem_v, cache_k, cache_v, page_table, ffn_norm, ffn_w13, ffn_w2, mix_norm, w_out, mem_norm, mem_w_kv, mem_q_norm, mem_k_norm, a_w_in, a_mu, a_w0, a_w1, a_w2, a_a0, a_a1, a_a2, a_g1, a_g2, a_k_k, a_k_a, a_r_k, a_lnx_w, a_lnx_b, kv_norm, kv_w, k_norm, b_w_in, b_q_norm, b_lam, b_subln):
    d = D_MODEL
    w13_b = ffn_w13.astype(BF16)
    w2_b = ffn_w2.astype(BF16)
    wout_b = w_out.astype(BF16)
    memw_b = mem_w_kv.astype(BF16)
    awin_b = a_w_in.astype(BF16)
    kvw_k, kvw_v = kv_w[:, :DIFF_W].astype(BF16), kv_w[:, DIFF_W:].astype(BF16)
    bq_b, bm_b = b_w_in[:, :, :DIFF_W].astype(BF16), b_w_in[:, :, DIFF_W:].astype(BF16)
    loras = []
    for i in range(N_A):
        w1, w2 = _pad_lora(a_w1[i], a_w2[i])
        a1, a2 = _pad_lora(a_a1[i], a_a2[i])
        loras.append(dict(w1=w1, w2=w2, a1=a1, a2=a2, g1=a_g1[i].astype(BF16), g2=a_g2[i].astype(BF16),
                          w0=a_w0[i].reshape(1, RWKV_W), a0=a_a0[i].reshape(1, RWKV_W),
                          k_k=a_k_k[i].reshape(1, RWKV_W), k_a=a_k_a[i].reshape(1, RWKV_W)))

    def run(x3, shift_prev, wkv0, mk, mv, pos, decode):
        bsz, t, _ = x3.shape
        m = bsz * t
        tm = min(512, m)
        x = x3.reshape(m, d)
        rope = rope_tables(jnp.broadcast_to(pos, (m,)) if decode else pos)
        rope_rows = m if decode else t
        shifts, states = [], []
        for i in range(N_A):
            x = ffn(x, ffn_norm[i, 0], w13_b, w2_b, i, 0, tm)
            x_shift = jnp.concatenate([jnp.zeros((bsz, 1, d), F32), x.reshape(bsz, t, d)[:, :-1]], axis=1)
            xn, cat, mix = a_prep(x, x_shift.reshape(m, d), shift_prev[i], mix_norm[i], a_mu[i], tm, t)
            shifts.append(xn.reshape(bsz, t, d)[:, -1])
            proj = mm(cat, awin_b, i, tm)
            ld, kp, kn, bvec, gate = a_mix(proj, mix, loras[i], min(256, m))
            proj3 = proj.reshape(bsz, t, A_IN)
            if decode:
                heads = lambda z: z.reshape(bsz, RWKV_HEADS, 1, RWKV_N)
                y, s_new = wkv_step(
                    wkv0[i], heads(proj[:, :RWKV_W]), heads(ld), heads(kp),
                    proj[:, 2 * RWKV_W:3 * RWKV_W].reshape(bsz, RWKV_HEADS, RWKV_N, 1), heads(kn), heads(bvec))
                y = y.reshape(m, RWKV_W)
            else:
                r3 = lambda z: z.reshape(bsz, t, RWKV_W)
                y, s_pairs = wkv_scan(proj3, r3(ld), r3(kp), r3(kn), r3(bvec))
                y = y.reshape(m, RWKV_W)
                s_new = jnp.stack([s_pairs[:, :, :RWKV_N, :RWKV_N], s_pairs[:, :, RWKV_N:, RWKV_N:]], axis=2)
                s_new = s_new.reshape(bsz, RWKV_HEADS, RWKV_N, RWKV_N)
            states.append(s_new)
            y_mix = a_post(y, proj, kp, gate, a_lnx_w[i], a_lnx_b[i], a_r_k[i], min(256, m))
            o_mem = mem_attn(proj3, 3 * RWKV_W // MEM_W, mk[i], mv[i], mem_q_norm[i], min(512, t))
            x = out_mm(x, y_mix, o_mem.reshape(m, MEM_W), wout_b, i, tm)
            x = ffn(x, ffn_norm[i, 1], w13_b, w2_b, i, 1, tm)
        if decode:
            k_rows = norm_mm(x, kv_norm, kvw_k, tm, 512, rope, k_norm, rope_rows)
            v_rows = norm_mm(x, kv_norm, kvw_v, tm, 512)
            k_col = k_rows.reshape(bsz, DIFF_HEADS, DIFF_DV, 1)
            v_row = v_rows.reshape(bsz, DIFF_HEADS, 1, DIFF_DV)
        else:
            k_rows, k_b = norm_mm(x, kv_norm, kvw_k, tm, 512, rope, k_norm, rope_rows, mxu_copy_scale=1.0)
            v_rows, v_b = norm_mm(x, kv_norm, kvw_v, tm, 512, mxu_copy_scale=1.0)
            k_b, v_b = k_b.reshape(bsz, t, DIFF_W), v_b.reshape(bsz, t, DIFF_W)
        for j in range(N_B):
            i = N_A + j
            lam_init = 0.8 - 0.6 * math.exp(-0.3 * i)
            x = ffn(x, ffn_norm[i, 0], w13_b, w2_b, i, 0, tm)
            q_mem = norm_mm(x, mix_norm[i], bm_b[j], tm, 512).reshape(bsz, t, MEM_W)
            if decode:
                q = norm_mm(x, mix_norm[i], bq_b[j], tm, 512, rope, b_q_norm[j], rope_rows)
                o = diff_attn_decode(q.reshape(bsz, DIFF_HEADS, DIFF_DV, 1), k_col, v_row, cache_kt, cache_vt,
                                     page_table, b_lam[j], b_subln[j], lam_init)
                o = o[:, :, 0, :].astype(BF16)
            else:
                _, q_b = norm_mm(x, mix_norm[i], bq_b[j], tm, 512, rope, b_q_norm[j], rope_rows,
                                 mxu_copy_scale=Q_SCALE)
                o = diff_attn_prompt(q_b.reshape(bsz, t, DIFF_W), k_b, v_b, b_lam[j], b_subln[j], lam_init)
            o_mem = mem_attn(q_mem, 0, mk[i], mv[i], mem_q_norm[i], min(512, t))
            x = out_mm(x, o.reshape(m, DIFF_W), o_mem.reshape(m, MEM_W), wout_b, i, tm)
            x = ffn(x, ffn_norm[i, 1], w13_b, w2_b, i, 1, tm)
        return (x.reshape(bsz, t, d), jnp.stack(states), jnp.stack(shifts),
                k_rows.reshape(bsz, t, DIFF_W), v_rows.reshape(bsz, t, DIFF_W))

    n_pool = cache_k.shape[0]
    cache_kt = jnp.transpose(cache_k, (0, 2, 3, 4, 1)).reshape(n_pool, DIFF_HEADS, DIFF_DV, PAGE_SIZE)
    cache_vt = jnp.transpose(cache_v, (0, 2, 1, 3))

    bp, tp, _ = x_prompt.shape
    bs, ts, _ = x_sample.shape
    assert ts == 1, "the sample group is decoded one token per sequence"
    mem2 = mem_prompt.reshape(bp * MEM_TOKENS, d)
    mk_p, mv_p = [], []
    for i in range(DEPTH):
        mk_i, mv_i = mem_kv(mem2, mem_norm[i], memw_b[i], mem_k_norm[i])
        mk_p.append(mk_i.reshape(bp, MEM_TOKENS, MEM_W))
        mv_p.append(mv_i.reshape(bp, MEM_TOKENS, MEM_W))

    y_p, wkv_p, shift_p, k_p, v_p = run(
        x_prompt, jnp.zeros((N_A, bp, d), F32), None, mk_p, mv_p, jnp.arange(tp), decode=False)
    mk_s = cache_mem_k.reshape(DEPTH, bs, MEM_TOKENS, MEM_W)
    mv_s = cache_mem_v.reshape(DEPTH, bs, MEM_TOKENS, MEM_W)
    y_s, wkv_s, shift_s, k_s, v_s = run(
        x_sample, state_shift, state_wkv, mk_s, mv_s, jnp.full((1,), PAST_LEN, jnp.int32), decode=True)

    kshape = lambda z, b_, t_: z.reshape(b_, t_, DIFF_HEADS, 2, DIFF_DH)
    vshape = lambda z, b_, t_: z.reshape(b_, t_, DIFF_HEADS, DIFF_DV)
    memshape = lambda zs: jnp.stack(zs).reshape(DEPTH, bp, MEM_TOKENS, MEM_HEADS, MEM_DH)
    return (y_p, y_s, wkv_p, shift_p, wkv_s, shift_s,
            kshape(k_p, bp, tp), vshape(v_p, bp, tp), kshape(k_s, bs, ts), vshape(v_s, bs, ts),
            memshape(mk_p), memshape(mv_p))
```

```python
import functools
import math

import jax
import jax.numpy as jnp
from jax import lax
from jax.experimental import pallas as pl
from jax.experimental.pallas import tpu as pltpu

F32 = jnp.float32
BF16 = jnp.bfloat16

D_MODEL = 2048
DEPTH = 4
N_A = 2
N_B = 2
MEM_TOKENS = 256
MEM_HEADS = 4
MEM_W = 512
MEM_DH = 128
RWKV_W = 1536
RWKV_N = 64
RWKV_HEADS = 24
LORA_PAD = 128
LORA_G = 256
A_IN = 3 * RWKV_W + MEM_W
DIFF_W = 1536
DIFF_DV = 128
DIFF_HEADS = 12
DIFF_DH = 64
ROT_DIM = 16
ROPE_THETA = 500000.0
D_FF = 5632
PAST_LEN = 16384
PAGE_SIZE = 128
NORM_EPS = 1e-6
LNX_EPS = 64e-5
SUBLN_EPS = 1e-5

LANES = 128
CHUNK = 64
VMEM_LIMIT = 60 * 1024 * 1024

NT_DIMS = (((1,), (1,)), ((), ()))
TN_DIMS = (((0,), (0,)), ((), ()))


def _params(*sem):
    return pltpu.CompilerParams(dimension_semantics=sem, vmem_limit_bytes=VMEM_LIMIT)


def _dot(a, b, dims=None):
    if dims is None:
        return jnp.dot(a, b, preferred_element_type=F32)
    return lax.dot_general(a, b, dims, preferred_element_type=F32)


def _rms(x, g, eps):
    return x * lax.rsqrt(jnp.mean(x * x, axis=-1, keepdims=True) + eps) * g


def _sigmoid(x):
    return 1.0 / (1.0 + jnp.exp(-x))


def _group_matrix(scale):
    r = lax.broadcasted_iota(jnp.int32, (LANES, LANES), 0) // RWKV_N
    c = lax.broadcasted_iota(jnp.int32, (LANES, LANES), 1) // RWKV_N
    return jnp.where(r == c, scale, 0.0).astype(BF16)


def _group_sum(x, gm):
    return _dot(x.astype(BF16), gm)


FFN_TF = 1408
FFN_SUB = (512, 512, 384)
assert sum(FFN_SUB) == FFN_TF and D_FF % FFN_TF == 0


def _ffn_kernel(x_ref, g_ref, w1_ref, w3_ref, w2_ref, o_ref, h_ref):
    f = pl.program_id(1)

    @pl.when(f == 0)
    def _():
        h_ref[...] = _rms(x_ref[...], g_ref[...], NORM_EPS).astype(BF16)
        o_ref[...] = jnp.zeros_like(o_ref)

    h = h_ref[...]
    start = 0
    for width in FFN_SUB:
        cols = slice(start, start + width)
        gate = _dot(h, w1_ref[:, cols])
        up = _dot(h, w3_ref[:, cols])
        act = (gate * _sigmoid(gate) * up).astype(BF16)
        o_ref[...] += _dot(act, w2_ref[cols, :])
        start += width

    @pl.when(f == pl.num_programs(1) - 1)
    def _():
        o_ref[...] = x_ref[...] + 0.5 * o_ref[...]


def ffn(x, g, w13, w2, layer, half, tm):
    m, d = x.shape
    tf = FFN_TF
    nf = D_FF // tf
    return pl.pallas_call(
        _ffn_kernel,
        grid=(m // tm, nf),
        in_specs=[
            pl.BlockSpec((tm, d), lambda i, f: (i, 0)),
            pl.BlockSpec((1, d), lambda i, f: (0, 0)),
            pl.BlockSpec((None, None, d, tf), lambda i, f: (layer, half, 0, f)),
            pl.BlockSpec((None, None, d, tf), lambda i, f: (layer, half, 0, f + nf)),
            pl.BlockSpec((None, None, tf, d), lambda i, f: (layer, half, f, 0)),
        ],
        out_specs=pl.BlockSpec((tm, d), lambda i, f: (i, 0)),
        out_shape=jax.ShapeDtypeStruct((m, d), F32),
        scratch_shapes=[pltpu.VMEM((tm, d), BF16)],
        compiler_params=_params("parallel", "arbitrary"),
        name="ffn",
    )(x, g.reshape(1, d), w13, w13, w2)


def _norm_mm_kernel(*refs, qk_epilogue, mxu_copy_scale):
    refs = list(refs)
    h_ref = refs.pop()
    ob_ref = refs.pop() if mxu_copy_scale is not None else None
    o_ref = refs.pop()
    x_ref, g_ref, w_ref = refs[:3]

    @pl.when(pl.program_id(1) == 0)
    def _():
        h_ref[...] = _rms(x_ref[...], g_ref[...], NORM_EPS).astype(BF16)

    def emit(cols, val):
        o_ref[:, cols] = val
        if ob_ref is not None:
            ob_ref[:, cols] = (val * mxu_copy_scale).astype(BF16)

    y = _dot(h_ref[...], w_ref[...])
    if not qk_epilogue:
        emit(slice(None), y)
        return
    gh_ref, cos_ref, s1_ref, s2_ref = refs[3:7]
    gm = _group_matrix(1.0 / DIFF_DH)
    gh, cos, s1, s2 = gh_ref[...], cos_ref[...], s1_ref[...], s2_ref[...]
    half = ROT_DIM // 2
    for j in range(y.shape[1] // LANES):
        blk = y[:, j * LANES:(j + 1) * LANES]
        nb = blk * lax.rsqrt(_group_sum(blk * blk, gm) + NORM_EPS) * gh
        emit(slice(j * LANES, (j + 1) * LANES),
             nb * cos + pltpu.roll(nb, LANES - half, 1) * s1 + pltpu.roll(nb, half, 1) * s2)


def norm_mm(x, g, w, tm, rope=None, head_gain=None, rows_per_seq=None, mxu_copy_scale=None):
    m, d = x.shape
    n = w.shape[1]
    tn = n
    qk = rope is not None
    in_specs = [
        pl.BlockSpec((tm, d), lambda i, j: (i, 0)),
        pl.BlockSpec((1, d), lambda i, j: (0, 0)),
        pl.BlockSpec((d, tn), lambda i, j: (0, j)),
    ]
    args = [x, g.reshape(1, d), w]
    if qk:
        tiles_per_seq = rows_per_seq // tm
        in_specs.append(pl.BlockSpec((1, LANES), lambda i, j: (0, 0)))
        args.append(jnp.tile(head_gain.reshape(1, DIFF_DH), (1, 2)))
        for t in rope:
            in_specs.append(pl.BlockSpec((tm, LANES), lambda i, j: (i % tiles_per_seq, 0)))
            args.append(t)
    out_spec = pl.BlockSpec((tm, tn), lambda i, j: (i, j))
    out_specs, out_shape = [out_spec], [jax.ShapeDtypeStruct((m, n), F32)]
    if mxu_copy_scale is not None:
        out_specs.append(out_spec)
        out_shape.append(jax.ShapeDtypeStruct((m, n), BF16))
    res = pl.pallas_call(
        functools.partial(_norm_mm_kernel, qk_epilogue=qk, mxu_copy_scale=mxu_copy_scale),
        grid=(m // tm, n // tn),
        in_specs=in_specs,
        out_specs=out_specs,
        out_shape=out_shape,
        scratch_shapes=[pltpu.VMEM((tm, d), BF16)],
        compiler_params=_params("parallel", "arbitrary"),
        name="norm_mm_qk" if qk else "norm_mm",
    )(*args)
    return res if mxu_copy_scale is not None else res[0]


def rope_tables(pos):
    half = ROT_DIM // 2
    inv = ROPE_THETA ** (-jnp.arange(0, ROT_DIM, 2, dtype=F32) / ROT_DIM)
    ang = pos.astype(F32)[:, None] * inv[None, :]
    cos, sin = jnp.cos(ang), jnp.sin(ang)
    t = pos.shape[0]
    rest = DIFF_DH - ROT_DIM
    c64 = jnp.concatenate([cos, cos, jnp.ones((t, rest), F32)], axis=1)
    s1_64 = jnp.concatenate([-sin, jnp.zeros((t, DIFF_DH - half), F32)], axis=1)
    s2_64 = jnp.concatenate([jnp.zeros((t, half), F32), sin, jnp.zeros((t, rest), F32)], axis=1)
    return tuple(jnp.tile(z, (1, 2)) for z in (c64, s1_64, s2_64))


def _mem_kv_kernel(x_ref, g_ref, w_ref, gk_ref, k_ref, v_ref):
    h = _rms(x_ref[...], g_ref[...], NORM_EPS).astype(BF16)
    y = _dot(h, w_ref[...])
    gk = gk_ref[...]
    for j in range(MEM_HEADS):
        blk = y[:, j * MEM_DH:(j + 1) * MEM_DH]
        k_ref[:, j * MEM_DH:(j + 1) * MEM_DH] = _rms(blk, gk, NORM_EPS)
    v_ref[...] = y[:, MEM_W:]


def mem_kv(mem, g, w, gk):
    m, d = mem.shape
    return pl.pallas_call(
        _mem_kv_kernel,
        grid=(1,),
        in_specs=[
            pl.BlockSpec((m, d), lambda i: (0, 0)),
            pl.BlockSpec((1, d), lambda i: (0, 0)),
            pl.BlockSpec((d, 2 * MEM_W), lambda i: (0, 0)),
            pl.BlockSpec((1, MEM_DH), lambda i: (0, 0)),
        ],
        out_specs=[pl.BlockSpec((m, MEM_W), lambda i: (0, 0))] * 2,
        out_shape=[jax.ShapeDtypeStruct((m, MEM_W), F32)] * 2,
        compiler_params=_params("arbitrary"),
        name="mem_kv",
    )(mem, g.reshape(1, d), w, gk.reshape(1, MEM_DH))


def _mem_attn_kernel(q_ref, k_ref, v_ref, gq_ref, o_ref):
    q = q_ref[0]
    rows = q.shape[0]
    if rows < 8:
        q = jnp.broadcast_to(q, (8, q.shape[1]))
    k = k_ref[0]
    v = v_ref[0]
    gq = gq_ref[...]
    for h in range(MEM_HEADS):
        sl = slice(h * MEM_DH, (h + 1) * MEM_DH)
        qh = _rms(q[:, sl], gq, NORM_EPS).astype(BF16)
        s = _dot(qh, k[:, sl].astype(BF16), NT_DIMS) * (MEM_DH ** -0.5)
        p = jnp.exp(s - jnp.max(s, axis=-1, keepdims=True))
        o = _dot(p.astype(BF16), v[:, sl].astype(BF16)) / jnp.sum(p, axis=-1, keepdims=True)
        o_ref[0, :, sl] = o[:rows].astype(o_ref.dtype)


def mem_attn(proj, q_col_block, mk, mv, gq, tq):
    b, t, _ = proj.shape
    return pl.pallas_call(
        _mem_attn_kernel,
        grid=(b, t // tq),
        in_specs=[
            pl.BlockSpec((1, tq, MEM_W), lambda i, j: (i, j, q_col_block)),
            pl.BlockSpec((1, MEM_TOKENS, MEM_W), lambda i, j: (i, 0, 0)),
            pl.BlockSpec((1, MEM_TOKENS, MEM_W), lambda i, j: (i, 0, 0)),
            pl.BlockSpec((1, MEM_DH), lambda i, j: (0, 0)),
        ],
        out_specs=pl.BlockSpec((1, tq, MEM_W), lambda i, j: (i, j, 0)),
        out_shape=jax.ShapeDtypeStruct((b, t, MEM_W), BF16),
        compiler_params=_params("parallel", "arbitrary"),
        name="mem_attn",
    )(proj, mk, mv, gq.reshape(1, MEM_DH))


def _out_mm_kernel(x_ref, a_ref, b_ref, wa_ref, wb_ref, o_ref):
    o_ref[...] = x_ref[...] + _dot(a_ref[...], wa_ref[...]) + _dot(b_ref[...], wb_ref[...])


def out_mm(x, a, b, w, layer, tm):
    m, d = x.shape
    tn = d
    ka, kb = a.shape[1], b.shape[1]
    kb_blocks = ka // kb
    return pl.pallas_call(
        _out_mm_kernel,
        grid=(m // tm, d // tn),
        in_specs=[
            pl.BlockSpec((tm, tn), lambda i, j: (i, j)),
            pl.BlockSpec((tm, ka), lambda i, j: (i, 0)),
            pl.BlockSpec((tm, kb), lambda i, j: (i, 0)),
            pl.BlockSpec((None, ka, tn), lambda i, j: (layer, 0, j)),
            pl.BlockSpec((None, kb, tn), lambda i, j: (layer, kb_blocks, j)),
        ],
        out_specs=pl.BlockSpec((tm, tn), lambda i, j: (i, j)),
        out_shape=jax.ShapeDtypeStruct((m, d), F32),
        compiler_params=_params("parallel", "arbitrary"),
        name="out_mm",
    )(x, a, b, w, w)


def _a_prep_kernel(x_ref, xs_ref, sp_ref, g_ref, mu_ref, xn_ref, cat_ref, mix_ref, *, tiles_per_seq):
    g = g_ref[...]
    xn = _rms(x_ref[...], g, NORM_EPS)
    if tiles_per_seq is None:
        xp = sp_ref[...]
    else:
        xp = _rms(xs_ref[...], g, NORM_EPS)
        first = pl.program_id(0) % tiles_per_seq == 0
        row = lax.broadcasted_iota(jnp.int32, xn.shape, 0)
        xp = jnp.where(jnp.logical_and(row == 0, first), sp_ref[0], xp)
    xx = xp - xn
    d = xn.shape[1]
    xn_ref[...] = xn
    cat_ref[:, :d] = xn.astype(BF16)
    cat_ref[:, d:] = xx.astype(BF16)
    for i in range(3):
        mix_ref[:, i * d:(i + 1) * d] = (xn + xx * mu_ref[i:i + 1, :]).astype(BF16)


def a_prep(x, x_shift, shift_prev, g, mu, tm, rows_per_seq):
    m, d = x.shape
    if rows_per_seq == 1:
        tiles_per_seq = None
        x_shift = x
        sp, sp_spec = shift_prev, pl.BlockSpec((tm, d), lambda i: (i, 0))
    else:
        tiles_per_seq = rows_per_seq // tm
        sp = shift_prev.reshape(shift_prev.shape[0], 1, d)
        sp_spec = pl.BlockSpec((1, 1, d), lambda i: (i // tiles_per_seq, 0, 0))
    return pl.pallas_call(
        functools.partial(_a_prep_kernel, tiles_per_seq=tiles_per_seq),
        grid=(m // tm,),
        in_specs=[
            pl.BlockSpec((tm, d), lambda i: (i, 0)),
            pl.BlockSpec((tm, d), lambda i: (i, 0)),
            sp_spec,
            pl.BlockSpec((1, d), lambda i: (0, 0)),
            pl.BlockSpec((3, d), lambda i: (0, 0)),
        ],
        out_specs=[
            pl.BlockSpec((tm, d), lambda i: (i, 0)),
            pl.BlockSpec((tm, 2 * d), lambda i: (i, 0)),
            pl.BlockSpec((tm, 3 * d), lambda i: (i, 0)),
        ],
        out_shape=[
            jax.ShapeDtypeStruct((m, d), F32),
            jax.ShapeDtypeStruct((m, 2 * d), BF16),
            jax.ShapeDtypeStruct((m, 3 * d), BF16),
        ],
        compiler_params=_params("parallel"),
        name="a_prep",
    )(x, x_shift, sp, g.reshape(1, d), mu)


def _mm_kernel(a_ref, w_ref, o_ref):
    o_ref[...] = _dot(a_ref[...], w_ref[...])


def mm(a, w, layer, tm, tn=1024):
    m, k = a.shape
    n = w.shape[2]
    return pl.pallas_call(
        _mm_kernel,
        grid=(m // tm, n // tn),
        in_specs=[pl.BlockSpec((tm, k), lambda i, j: (i, 0)),
                  pl.BlockSpec((None, k, tn), lambda i, j: (layer, 0, j))],
        out_specs=pl.BlockSpec((tm, tn), lambda i, j: (i, j)),
        out_shape=jax.ShapeDtypeStruct((m, n), F32),
        compiler_params=_params("parallel", "arbitrary"),
        name="mm",
    )(a, w)


def _a_mix_kernel(k_ref, mix_ref, w1_ref, w2_ref, a1_ref, a2_ref, g1_ref, g2_ref, w0_ref, a0_ref, kk_ref,
                  ka_ref, ld_ref, kp_ref, kn_ref, b_ref, g_ref):
    d = D_MODEL
    xw, xa, xg = mix_ref[:, :d], mix_ref[:, d:2 * d], mix_ref[:, 2 * d:]
    wl = _dot(jnp.tanh(_dot(xw, w1_ref[...])).astype(BF16), w2_ref[...]) + w0_ref[...]
    w = -(jnp.maximum(-wl, 0.0) + jnp.log(1.0 + jnp.exp(-jnp.abs(wl)))) - 0.5
    ld_ref[...] = -jnp.exp(w)
    a = _sigmoid(_dot(_dot(xa, a1_ref[...]).astype(BF16), a2_ref[...]) + a0_ref[...])
    g_ref[...] = _dot(_sigmoid(_dot(xg, g1_ref[...])).astype(BF16), g2_ref[...])
    k = k_ref[...]
    kp_ref[...] = k * (1.0 + (a - 1.0) * ka_ref[...])
    kraw = k * kk_ref[...]
    gm = _group_matrix(1.0)
    for j in range(RWKV_W // LANES):
        sl = slice(j * LANES, (j + 1) * LANES)
        blk = kraw[:, sl]
        kn = blk / jnp.maximum(jnp.sqrt(_group_sum(blk * blk, gm)), 1e-12)
        kn_ref[:, sl] = kn
        b_ref[:, sl] = kn * a[:, sl]


def a_mix(proj, mix, lw, tm):
    m = proj.shape[0]
    row = lambda i: (i, 0)
    fixed = lambda i: (0, 0)
    vec = pl.BlockSpec((1, RWKV_W), fixed)
    out = jax.ShapeDtypeStruct((m, RWKV_W), F32)
    return pl.pallas_call(
        _a_mix_kernel,
        grid=(m // tm,),
        in_specs=[
            pl.BlockSpec((tm, RWKV_W), lambda i: (i, 1)),
            pl.BlockSpec((tm, 3 * D_MODEL), row),
            pl.BlockSpec((D_MODEL, LORA_PAD), fixed), pl.BlockSpec((LORA_PAD, RWKV_W), fixed),
            pl.BlockSpec((D_MODEL, LORA_PAD), fixed), pl.BlockSpec((LORA_PAD, RWKV_W), fixed),
            pl.BlockSpec((D_MODEL, LORA_G), fixed), pl.BlockSpec((LORA_G, RWKV_W), fixed),
            vec, vec, vec, vec,
        ],
        out_specs=[pl.BlockSpec((tm, RWKV_W), row)] * 5,
        out_shape=[out] * 5,
        compiler_params=_params("parallel"),
        name="a_mix",
    )(proj, mix, lw["w1"], lw["w2"], lw["a1"], lw["a2"], lw["g1"], lw["g2"],
      lw["w0"], lw["a0"], lw["k_k"], lw["k_a"])


def _wkv_masks(c):
    c2 = 2 * c
    ri = lax.broadcasted_iota(jnp.int32, (c2, c2), 0)
    ci = lax.broadcasted_iota(jnp.int32, (c2, c2), 1)
    same = (ri // c) == (ci // c)
    masks = dict(
        in_head0=lax.broadcasted_iota(jnp.int32, (c, LANES), 1) < RWKV_N,
        tri=jnp.where(lax.broadcasted_iota(jnp.int32, (c, c), 0) >= lax.broadcasted_iota(jnp.int32, (c, c), 1),
                      1.0, 0.0).astype(BF16),
        strict=jnp.logical_and(same, (ri % c) > (ci % c)),
        incl=jnp.logical_and(same, (ri % c) >= (ci % c)),
        eye=jnp.where(ri == ci, 1.0, 0.0),
        diag8=(ri // 8) == (ci // 8),
        lower_left=[],
    )
    size = 8
    while size < c:
        masks["lower_left"].append(jnp.logical_and(
            (ri // (2 * size)) == (ci // (2 * size)),
            jnp.logical_and((ri // size) % 2 == 1, (ci // size) % 2 == 0)))
        size *= 2
    return masks


def _each(f, *lists):
    return [f(*args) for args in zip(*lists)]


def _wkv_chunk(s, r, ld, k, v, kn, b, mk):
    c = r[0].shape[0]
    c2 = 2 * c
    in_head0 = mk["in_head0"]
    tri = mk["tri"]

    def stack(z):
        return jnp.concatenate([jnp.where(in_head0, z, 0.0), jnp.where(in_head0, 0.0, z)], axis=0)

    def bdot(x, y):
        return _dot(x.astype(BF16), y.astype(BF16))

    p1 = _each(lambda z: z.astype(BF16), ld)
    rem = _each(lambda z, p: z - p.astype(F32), ld, p1)
    p2 = _each(lambda z: z.astype(BF16), rem)
    p3 = _each(lambda z, p: (z - p.astype(F32)).astype(BF16), rem, p2)
    cum = _each(lambda a1, a2, a3: _dot(tri, a1) + _dot(tri, a2) + _dot(tri, a3), p1, p2, p3)
    eg = _each(jnp.exp, cum)
    einv = _each(lambda z: jnp.exp(-z), cum)
    at_s = _each(lambda n, z, d: stack(-n * jnp.exp(z - d)).astype(BF16), kn, cum, ld)
    rt_s = _each(lambda x, e: stack(x * e).astype(BF16), r, eg)
    bt = _each(lambda x, e: x * e, b, einv)
    kt = _each(lambda x, e: x * e, k, einv)
    v_s = _each(lambda x: stack(x).astype(BF16), v)

    gmat = _each(lambda a, x, y, z: _dot(jnp.concatenate([a, x], axis=0),
                                         jnp.concatenate([y, y, z, z], axis=0).astype(BF16), NT_DIMS),
                 at_s, rt_s, bt, kt)
    a_ab = _each(lambda g: jnp.where(mk["strict"], g[:c2, :c2], 0.0), gmat)
    a_ak = _each(lambda g: jnp.where(mk["strict"], g[:c2, c2:], 0.0).astype(BF16), gmat)
    a_rr = _each(lambda g: jnp.concatenate([jnp.where(mk["incl"], g[c2:, :c2], 0.0),
                                            jnp.where(mk["incl"], g[c2:, c2:], 0.0)], axis=1).astype(BF16), gmat)

    ad = _each(lambda a: jnp.where(mk["diag8"], a, 0.0), a_ab)
    ad2 = _each(bdot, ad, ad)
    ad4 = _each(bdot, ad2, ad2)
    inv = _each(lambda a: mk["eye"] + a, ad)
    inv = _each(lambda x, y: x + bdot(x, y), inv, ad2)
    inv = _each(lambda x, y: x + bdot(x, y), inv, ad4)
    for lower_left in mk["lower_left"]:
        half = _each(lambda x, a: bdot(x, jnp.where(lower_left, a, 0.0)), inv, a_ab)
        inv = _each(lambda x, y: x + bdot(y, x), inv, half)

    s_b = _each(lambda z: z.astype(BF16), s)
    rhs = _each(lambda a, x, y, z: _dot(a, x) + _dot(y, z, NT_DIMS), a_ak, v_s, at_s, s_b)
    u = _each(lambda x, y: bdot(x, y).astype(BF16), inv, rhs)
    uv = _each(lambda x, y: jnp.concatenate([x, y], axis=0), u, v_s)
    y_s = _each(lambda x, z, a, w: _dot(x, z, NT_DIMS) + _dot(a, w), rt_s, s_b, a_rr, uv)
    ds = _each(lambda w, x, y: _dot(w, jnp.concatenate([stack(x), stack(y)], axis=0).astype(BF16), TN_DIMS),
               uv, bt, kt)
    y = _each(lambda z: z[:c] + z[c:], y_s)
    s_new = _each(lambda z, dz, e: (z + dz) * e[c - 1:c, :], s, ds, eg)
    return y, s_new


def _wkv_kernel(r_ref, ld_ref, k_ref, v_ref, kn_ref, b_ref, y_ref, s_ref, st_ref, *, chunks, pairs):
    @pl.when(pl.program_id(2) == 0)
    def _():
        st_ref[...] = jnp.zeros_like(st_ref)

    mk = _wkv_masks(CHUNK)
    s = [st_ref[p] for p in range(pairs)]
    for i in range(chunks):
        rows = slice(i * CHUNK, (i + 1) * CHUNK)
        cut = lambda ref: [ref[0, rows, p * LANES:(p + 1) * LANES] for p in range(pairs)]
        y, s = _wkv_chunk(s, cut(r_ref), cut(ld_ref), cut(k_ref), cut(v_ref), cut(kn_ref), cut(b_ref), mk)
        for p in range(pairs):
            y_ref[0, rows, p * LANES:(p + 1) * LANES] = y[p]
    for p in range(pairs):
        st_ref[p] = s[p]

    @pl.when(pl.program_id(2) == pl.num_programs(2) - 1)
    def _():
        s_ref[0] = st_ref[...]


WKV_PAIRS = 12
WKV_CHUNKS = 2


def wkv_scan(proj, ld, kp, kn, b):
    bsz, t, _ = proj.shape
    groups = RWKV_W // LANES // WKV_PAIRS
    tb = WKV_CHUNKS * CHUNK
    width = WKV_PAIRS * LANES
    blk = lambda off: pl.BlockSpec((1, tb, width), lambda i, p, c: (i, c, p + off))
    return pl.pallas_call(
        functools.partial(_wkv_kernel, chunks=WKV_CHUNKS, pairs=WKV_PAIRS),
        grid=(bsz, groups, t // tb),
        in_specs=[blk(0), blk(0), blk(0), blk(2 * groups), blk(0), blk(0)],
        out_specs=[
            pl.BlockSpec((1, tb, width), lambda i, p, c: (i, c, p)),
            pl.BlockSpec((1, WKV_PAIRS, LANES, LANES), lambda i, p, c: (i, p, 0, 0)),
        ],
        out_shape=[
            jax.ShapeDtypeStruct((bsz, t, RWKV_W), F32),
            jax.ShapeDtypeStruct((bsz, RWKV_W // LANES, LANES, LANES), F32),
        ],
        scratch_shapes=[pltpu.VMEM((WKV_PAIRS, LANES, LANES), F32)],
        compiler_params=_params("parallel", "parallel", "arbitrary"),
        name="wkv_scan",
    )(proj, ld, kp, proj, kn, b)


def _wkv_step_kernel(s_ref, r_ref, ld_ref, k_ref, v_ref, kn_ref, b_ref, y_ref, so_ref):
    s = s_ref[0]
    sa = jnp.sum(s * (-kn_ref[0]), axis=-1, keepdims=True)
    s = s * jnp.exp(ld_ref[0]) + sa * b_ref[0] + v_ref[0] * k_ref[0]
    so_ref[0] = s
    y_ref[0] = jnp.sum(s * r_ref[0], axis=-1, keepdims=True)


def wkv_step(s0, r, ld, k, v, kn, b):
    bsz = s0.shape[0]
    h, n = RWKV_HEADS, RWKV_N
    st = pl.BlockSpec((1, h, n, n), lambda i: (i, 0, 0, 0))
    rw = pl.BlockSpec((1, h, 1, n), lambda i: (i, 0, 0, 0))
    cl = pl.BlockSpec((1, h, n, 1), lambda i: (i, 0, 0, 0))
    return pl.pallas_call(
        _wkv_step_kernel,
        grid=(bsz,),
        in_specs=[st, rw, rw, rw, cl, rw, rw],
        out_specs=[cl, st],
        out_shape=[jax.ShapeDtypeStruct((bsz, h, n, 1), F32), jax.ShapeDtypeStruct((bsz, h, n, n), F32)],
        compiler_params=_params("parallel"),
        name="wkv_step",
    )(s0, r, ld, k, v, kn, b)


def _a_post_kernel(y_ref, r_ref, kp_ref, v_ref, g_ref, lw_ref, lb_ref, rk_ref, o_ref):
    gsum = _group_matrix(1.0)
    gmean = _group_matrix(1.0 / RWKV_N)
    for j in range(RWKV_W // LANES):
        sl = slice(j * LANES, (j + 1) * LANES)
        y = y_ref[:, sl]
        cen = y - _group_sum(y, gmean)
        yn = cen * lax.rsqrt(_group_sum(cen * cen, gmean) + LNX_EPS) * lw_ref[:, sl] + lb_ref[:, sl]
        bonus = _group_sum(r_ref[:, sl] * kp_ref[:, sl] * rk_ref[:, sl], gsum) * v_ref[:, sl]
        o_ref[:, sl] = ((yn + bonus) * g_ref[:, sl]).astype(o_ref.dtype)


def a_post(y, proj, kp, g, lnx_w, lnx_b, r_k, tm):
    m = y.shape[0]
    row = lambda i: (i, 0)
    vec = pl.BlockSpec((1, RWKV_W), lambda i: (0, 0))
    tile = pl.BlockSpec((tm, RWKV_W), row)
    return pl.pallas_call(
        _a_post_kernel,
        grid=(m // tm,),
        in_specs=[tile, pl.BlockSpec((tm, RWKV_W), lambda i: (i, 0)), tile,
                  pl.BlockSpec((tm, RWKV_W), lambda i: (i, 2)), tile, vec, vec, vec],
        out_specs=tile,
        out_shape=jax.ShapeDtypeStruct((m, RWKV_W), BF16),
        compiler_params=_params("parallel"),
        name="a_post",
    )(y, proj, kp, proj, g, lnx_w.reshape(1, RWKV_W), lnx_b.reshape(1, RWKV_W), r_k.reshape(1, RWKV_W))


def _lambda(lam_ref, lam_init):
    lq = lam_ref[...]
    l1 = jnp.sum(lq[0:1] * lq[1:2], axis=-1, keepdims=True)
    l2 = jnp.sum(lq[2:3] * lq[3:4], axis=-1, keepdims=True)
    return jnp.exp(l1) - jnp.exp(l2) + lam_init


ATTN_HEADS = 2
ATTN_TQ = 512
LOG2E = 1.4426950408889634
Q_SCALE = DIFF_DH ** -0.5 * LOG2E


def _diff_attn_kernel(qi_ref, ki_ref, q_ref, k_ref, v_ref, lam_ref, gs_ref, o_ref, m_ref, l_ref, acc_ref, *,
                      tq, lam_init):
    qi = qi_ref[pl.program_id(2)]
    ki = ki_ref[pl.program_id(2)]

    @pl.when(ki == 0)
    def _():
        m_ref[...] = jnp.full_like(m_ref, -jnp.inf)
        l_ref[...] = jnp.zeros_like(l_ref)
        acc_ref[...] = jnp.zeros_like(acc_ref)

    def step(diagonal):
        lane = lax.broadcasted_iota(jnp.int32, (tq, DIFF_DV), 1)
        if diagonal:
            visible = (lax.broadcasted_iota(jnp.int32, (tq, tq), 1)
                       <= lax.broadcasted_iota(jnp.int32, (tq, tq), 0))
        for h in range(ATTN_HEADS):
            cols = slice(h * DIFF_DV, (h + 1) * DIFF_DV)
            q = q_ref[0, :, cols]
            kb = k_ref[0, :, cols]
            vb = v_ref[0, :, cols]
            for c in range(2):
                idx = 2 * h + c
                in_comp = (lane < DIFF_DH) if c == 0 else (lane >= DIFF_DH)
                s = _dot(jnp.where(in_comp, q, jnp.zeros_like(q)), kb, NT_DIMS)
                if diagonal:
                    s = jnp.where(visible, s, -jnp.inf)
                m_old = m_ref[idx]
                m_new = jnp.maximum(m_old, jnp.max(s, axis=-1, keepdims=True))
                alpha = jnp.exp2(m_old - m_new)
                p = jnp.exp2(s - jnp.concatenate([m_new] * (tq // LANES), axis=1))
                l_ref[idx] = alpha * l_ref[idx] + jnp.sum(p, axis=-1, keepdims=True)
                acc_ref[idx] = alpha * acc_ref[idx] + _dot(p.astype(BF16), vb)
                m_ref[idx] = m_new

    @pl.when(ki < qi)
    def _():
        step(False)

    @pl.when(ki == qi)
    def _():
        step(True)
        lam = _lambda(lam_ref, lam_init)
        for h in range(ATTN_HEADS):
            o = acc_ref[2 * h] / l_ref[2 * h] - lam * (acc_ref[2 * h + 1] / l_ref[2 * h + 1])
            o_ref[0, :, h * DIFF_DV:(h + 1) * DIFF_DV] = (
                _rms(o, gs_ref[...], SUBLN_EPS) * (1.0 - lam_init)).astype(o_ref.dtype)


def diff_attn_prompt(q, k, v, lam_p, subln, lam_init):
    b, t, _ = q.shape
    tq = ATTN_TQ
    width = ATTN_HEADS * DIFF_DV
    chains = 2 * ATTN_HEADS
    blocks = t // tq
    pairs = [(qi, ki) for qi in range(blocks) for ki in range(qi + 1)]
    q_of = jnp.asarray([p[0] for p in pairs], jnp.int32)
    k_of = jnp.asarray([p[1] for p in pairs], jnp.int32)
    q_map = lambda i, h, p, q_of, k_of: (i, q_of[p], h)
    kv_map = lambda i, h, p, q_of, k_of: (i, k_of[p], h)
    fixed = lambda i, h, p, q_of, k_of: (0, 0)
    grid_spec = pltpu.PrefetchScalarGridSpec(
        num_scalar_prefetch=2,
        grid=(b, DIFF_HEADS // ATTN_HEADS, len(pairs)),
        in_specs=[
            pl.BlockSpec((1, tq, width), q_map),
            pl.BlockSpec((1, tq, width), kv_map),
            pl.BlockSpec((1, tq, width), kv_map),
            pl.BlockSpec((4, DIFF_DH), fixed),
            pl.BlockSpec((1, DIFF_DV), fixed),
        ],
        out_specs=pl.BlockSpec((1, tq, width), q_map),
        scratch_shapes=[pltpu.VMEM((chains, tq, LANES), F32), pltpu.VMEM((chains, tq, LANES), F32),
                        pltpu.VMEM((chains, tq, DIFF_DV), F32)],
    )
    return pl.pallas_call(
        functools.partial(_diff_attn_kernel, tq=tq, lam_init=lam_init),
        grid_spec=grid_spec,
        out_shape=jax.ShapeDtypeStruct((b, t, DIFF_W), BF16),
        compiler_params=_params("parallel", "parallel", "arbitrary"),
        name="diff_attn_prompt",
    )(q_of, k_of, q, k, v, lam_p, subln.reshape(1, DIFF_DV))


DEC_PAGES = 4
DEC_ROWS = 8


def _dec_attn_kernel(pt_ref, q_ref, kn_ref, vn_ref, lam_ref, gs_ref, *refs, lam_init):
    k_refs = refs[:DEC_PAGES]
    v_refs = refs[DEC_PAGES:2 * DEC_PAGES]
    o_ref, qb_ref, m_ref, l_ref, acc_ref = refs[2 * DEC_PAGES:]
    step = pl.program_id(1)
    row = lax.broadcasted_iota(jnp.int32, (DIFF_HEADS, DEC_ROWS, LANES), 1)

    def comp_rows(prod):
        s0 = jnp.sum(prod[:, :DIFF_DH, :], axis=1, keepdims=True)
        s1 = jnp.sum(prod[:, DIFF_DH:, :], axis=1, keepdims=True)
        return jnp.where(row == 0, s0, jnp.where(row == 1, s1, 0.0))

    @pl.when(step == 0)
    def _():
        q_col = q_ref[0] * (DIFF_DH ** -0.5)
        qb_ref[...] = jnp.broadcast_to(q_col, qb_ref.shape)
        m_ref[...] = comp_rows(q_col * kn_ref[0])
        l_ref[...] = jnp.ones_like(l_ref)
        acc_ref[...] = jnp.broadcast_to(vn_ref[0], acc_ref.shape)

    qb = qb_ref[...]
    for k_ref, v_ref in zip(k_refs, v_refs):
        s = comp_rows(k_ref[0] * qb)
        m_old = m_ref[...]
        m_new = jnp.maximum(m_old, jnp.max(s, axis=-1, keepdims=True))
        alpha = jnp.exp(m_old - m_new)
        p = jnp.exp(s - m_new)
        l_ref[...] = alpha * l_ref[...] + jnp.sum(p, axis=-1, keepdims=True)
        pv = lax.dot_general(p.astype(BF16), v_ref[0].astype(BF16), (((2,), (1,)), ((0,), (0,))),
                             preferred_element_type=F32)
        acc_ref[...] = alpha * acc_ref[...] + pv
        m_ref[...] = m_new

    @pl.when(step == pl.num_programs(1) - 1)
    def _():
        lam = _lambda(lam_ref, lam_init)
        w = acc_ref[...] / l_ref[...]
        o = w[:, 0:1, :] - lam * w[:, 1:2, :]
        o = o * lax.rsqrt(jnp.mean(o * o, axis=-1, keepdims=True) + SUBLN_EPS) * gs_ref[...] * (1.0 - lam_init)
        o_ref[0] = jnp.broadcast_to(o, o_ref.shape[1:])


def diff_attn_decode(q_col, k_col, v_new, cache_kt, cache_v, page_table, lam_p, subln, lam_init):
    b = q_col.shape[0]
    n_pages = page_table.shape[1]
    col = pl.BlockSpec((1, DIFF_HEADS, DIFF_DV, 1), lambda i, s, pt: (i, 0, 0, 0))
    state = pltpu.VMEM((DIFF_HEADS, DEC_ROWS, LANES), F32)

    def page_spec(j):
        return pl.BlockSpec((1, DIFF_HEADS, PAGE_SIZE, DIFF_DV), lambda i, s, pt: (pt[i, s * DEC_PAGES + j], 0, 0, 0))

    grid_spec = pltpu.PrefetchScalarGridSpec(
        num_scalar_prefetch=1,
        grid=(b, n_pages // DEC_PAGES),
        in_specs=[col, col,
                  pl.BlockSpec((1, DIFF_HEADS, 1, DIFF_DV), lambda i, s, pt: (i, 0, 0, 0)),
                  pl.BlockSpec((4, DIFF_DH), lambda i, s, pt: (0, 0)),
                  pl.BlockSpec((1, DIFF_DV), lambda i, s, pt: (0, 0))]
        + [page_spec(j) for j in range(DEC_PAGES)] * 2,
        out_specs=pl.BlockSpec((1, DIFF_HEADS, DEC_ROWS, DIFF_DV), lambda i, s, pt: (i, 0, 0, 0)),
        scratch_shapes=[pltpu.VMEM((DIFF_HEADS, DIFF_DV, PAGE_SIZE), F32), state, state, state],
    )
    return pl.pallas_call(
        functools.partial(_dec_attn_kernel, lam_init=lam_init),
        grid_spec=grid_spec,
        out_shape=jax.ShapeDtypeStruct((b, DIFF_HEADS, DEC_ROWS, DIFF_DV), F32),
        compiler_params=_params("parallel", "arbitrary"),
        name="diff_attn_decode",
    )(page_table, q_col, k_col, v_new, lam_p, subln.reshape(1, DIFF_DV),
      *([cache_kt] * DEC_PAGES), *([cache_v] * DEC_PAGES))


def _pad_lora(w_in, w_out):
    pad = LORA_PAD - w_in.shape[1]
    return (jnp.pad(w_in, ((0, 0), (0, pad))).astype(BF16), jnp.pad(w_out, ((0, pad), (0, 0))).astype(BF16))


def kernel(x_prompt, x_sample, mem_prompt, state_wkv, state_shift, cache_mem_k, cache_mem_v, cache_k, cache_v, page_table, ffn_norm, ffn_w13, ffn_w2, mix_norm, w_out, mem_norm, mem_w_kv, mem_q_norm, mem_k_norm, a_w_in, a_mu, a_w0, a_w1, a_w2, a_a0, a_a1, a_a2, a_g1, a_g2, a_k_k, a_k_a, a_r_k, a_lnx_w, a_lnx_b, kv_norm, kv_w, k_norm, b_w_in, b_q_norm, b_lam, b_subln):
    d = D_MODEL
    w13_b = ffn_w13.astype(BF16)
    w2_b = ffn_w2.astype(BF16)
    wout_b = w_out.astype(BF16)
    memw_b = mem_w_kv.astype(BF16)
    awin_b = a_w_in.astype(BF16)
    kvw_k, kvw_v = kv_w[:, :DIFF_W].astype(BF16), kv_w[:, DIFF_W:].astype(BF16)
    bq_b, bm_b = b_w_in[:, :, :DIFF_W].astype(BF16), b_w_in[:, :, DIFF_W:].astype(BF16)
    loras = []
    for i in range(N_A):
        w1, w2 = _pad_lora(a_w1[i], a_w2[i])
        a1, a2 = _pad_lora(a_a1[i], a_a2[i])
        loras.append(dict(w1=w1, w2=w2, a1=a1, a2=a2, g1=a_g1[i].astype(BF16), g2=a_g2[i].astype(BF16),
                          w0=a_w0[i].reshape(1, RWKV_W), a0=a_a0[i].reshape(1, RWKV_W),
                          k_k=a_k_k[i].reshape(1, RWKV_W), k_a=a_k_a[i].reshape(1, RWKV_W)))

    def run(x3, shift_prev, wkv0, mk, mv, pos, decode):
        bsz, t, _ = x3.shape
        m = bsz * t
        tm = min(512, m)
        x = x3.reshape(m, d)
        rope = rope_tables(jnp.broadcast_to(pos, (m,)) if decode else pos)
        rope_rows = m if decode else t
        shifts, states = [], []
        for i in range(N_A):
            x = ffn(x, ffn_norm[i, 0], w13_b, w2_b, i, 0, tm)
            x_shift = jnp.concatenate([jnp.zeros((bsz, 1, d), F32), x.reshape(bsz, t, d)[:, :-1]], axis=1)
            xn, cat, mix = a_prep(x, x_shift.reshape(m, d), shift_prev[i], mix_norm[i], a_mu[i], tm, t)
            shifts.append(xn.reshape(bsz, t, d)[:, -1])
            proj = mm(cat, awin_b, i, tm)
            ld, kp, kn, bvec, gate = a_mix(proj, mix, loras[i], min(256, m))
            proj3 = proj.reshape(bsz, t, A_IN)
            if decode:
                heads = lambda z: z.reshape(bsz, RWKV_HEADS, 1, RWKV_N)
                y, s_new = wkv_step(
                    wkv0[i], heads(proj[:, :RWKV_W]), heads(ld), heads(kp),
                    proj[:, 2 * RWKV_W:3 * RWKV_W].reshape(bsz, RWKV_HEADS, RWKV_N, 1), heads(kn), heads(bvec))
                y = y.reshape(m, RWKV_W)
            else:
                r3 = lambda z: z.reshape(bsz, t, RWKV_W)
                y, s_pairs = wkv_scan(proj3, r3(ld), r3(kp), r3(kn), r3(bvec))
                y = y.reshape(m, RWKV_W)
                s_new = jnp.stack([s_pairs[:, :, :RWKV_N, :RWKV_N], s_pairs[:, :, RWKV_N:, RWKV_N:]], axis=2)
                s_new = s_new.reshape(bsz, RWKV_HEADS, RWKV_N, RWKV_N)
            states.append(s_new)
            y_mix = a_post(y, proj, kp, gate, a_lnx_w[i], a_lnx_b[i], a_r_k[i], min(256, m))
            o_mem = mem_attn(proj3, 3 * RWKV_W // MEM_W, mk[i], mv[i], mem_q_norm[i], min(512, t))
            x = out_mm(x, y_mix, o_mem.reshape(m, MEM_W), wout_b, i, tm)
            x = ffn(x, ffn_norm[i, 1], w13_b, w2_b, i, 1, tm)
        if decode:
            k_rows = norm_mm(x, kv_norm, kvw_k, tm, rope, k_norm, rope_rows)
            v_rows = norm_mm(x, kv_norm, kvw_v, tm)
            k_col = k_rows.reshape(bsz, DIFF_HEADS, DIFF_DV, 1)
            v_row = v_rows.reshape(bsz, DIFF_HEADS, 1, DIFF_DV)
        else:
            k_rows, k_b = norm_mm(x, kv_norm, kvw_k, tm, rope, k_norm, rope_rows, mxu_copy_scale=1.0)
            v_rows, v_b = norm_mm(x, kv_norm, kvw_v, tm, mxu_copy_scale=1.0)
            k_b, v_b = k_b.reshape(bsz, t, DIFF_W), v_b.reshape(bsz, t, DIFF_W)
        for j in range(N_B):
            i = N_A + j
            lam_init = 0.8 - 0.6 * math.exp(-0.3 * i)
            x = ffn(x, ffn_norm[i, 0], w13_b, w2_b, i, 0, tm)
            q_mem = norm_mm(x, mix_norm[i], bm_b[j], tm).reshape(bsz, t, MEM_W)
            if decode:
                q = norm_mm(x, mix_norm[i], bq_b[j], tm, rope, b_q_norm[j], rope_rows)
                o = diff_attn_decode(q.reshape(bsz, DIFF_HEADS, DIFF_DV, 1), k_col, v_row, cache_kt, cache_vt,
                                     page_table, b_lam[j], b_subln[j], lam_init)
                o = o[:, :, 0, :].astype(BF16)
            else:
                _, q_b = norm_mm(x, mix_norm[i], bq_b[j], tm, rope, b_q_norm[j], rope_rows,
                                 mxu_copy_scale=Q_SCALE)
                o = diff_attn_prompt(q_b.reshape(bsz, t, DIFF_W), k_b, v_b, b_lam[j], b_subln[j], lam_init)
            o_mem = mem_attn(q_mem, 0, mk[i], mv[i], mem_q_norm[i], min(512, t))
            x = out_mm(x, o.reshape(m, DIFF_W), o_mem.reshape(m, MEM_W), wout_b, i, tm)
            x = ffn(x, ffn_norm[i, 1], w13_b, w2_b, i, 1, tm)
        return (x.reshape(bsz, t, d), jnp.stack(states), jnp.stack(shifts),
                k_rows.reshape(bsz, t, DIFF_W), v_rows.reshape(bsz, t, DIFF_W))

    n_pool = cache_k.shape[0]
    cache_kt = jnp.transpose(cache_k, (0, 2, 3, 4, 1)).reshape(n_pool, DIFF_HEADS, DIFF_DV, PAGE_SIZE)
    cache_vt = jnp.transpose(cache_v, (0, 2, 1, 3))

    bp, tp, _ = x_prompt.shape
    bs, ts, _ = x_sample.shape
    assert ts == 1, "the sample group is decoded one token per sequence"
    mem2 = mem_prompt.reshape(bp * MEM_TOKENS, d)
    mk_p, mv_p = [], []
    for i in range(DEPTH):
        mk_i, mv_i = mem_kv(mem2, mem_norm[i], memw_b[i], mem_k_norm[i])
        mk_p.append(mk_i.reshape(bp, MEM_TOKENS, MEM_W))
        mv_p.append(mv_i.reshape(bp, MEM_TOKENS, MEM_W))

    y_p, wkv_p, shift_p, k_p, v_p = run(
        x_prompt, jnp.zeros((N_A, bp, d), F32), None, mk_p, mv_p, jnp.arange(tp), decode=False)
    mk_s = cache_mem_k.reshape(DEPTH, bs, MEM_TOKENS, MEM_W)
    mv_s = cache_mem_v.reshape(DEPTH, bs, MEM_TOKENS, MEM_W)
    y_s, wkv_s, shift_s, k_s, v_s = run(
        x_sample, state_shift, state_wkv, mk_s, mv_s, jnp.full((1,), PAST_LEN, jnp.int32), decode=True)

    kshape = lambda z, b_, t_: z.reshape(b_, t_, DIFF_HEADS, 2, DIFF_DH)
    vshape = lambda z, b_, t_: z.reshape(b_, t_, DIFF_HEADS, DIFF_DV)
    memshape = lambda zs: jnp.stack(zs).reshape(DEPTH, bp, MEM_TOKENS, MEM_HEADS, MEM_DH)
    return (y_p, y_s, wkv_p, shift_p, wkv_s, shift_s,
            kshape(k_p, bp, tp), vshape(v_p, bp, tp), kshape(k_s, bs, ts), vshape(v_s, bs, ts),
            memshape(mk_p), memshape(mv_p))
```

```python
import functools
import math

import jax
import jax.numpy as jnp
from jax import lax
from jax.experimental import pallas as pl
from jax.experimental.pallas import tpu as pltpu

F32 = jnp.float32
BF16 = jnp.bfloat16

D_MODEL = 2048
DEPTH = 4
N_A = 2
N_B = 2
MEM_TOKENS = 256
MEM_HEADS = 4
MEM_W = 512
MEM_DH = 128
RWKV_W = 1536
RWKV_N = 64
RWKV_HEADS = 24
LORA_PAD = 128
LORA_G = 256
A_IN = 3 * RWKV_W + MEM_W
DIFF_W = 1536
DIFF_DV = 128
DIFF_HEADS = 12
DIFF_DH = 64
ROT_DIM = 16
ROPE_THETA = 500000.0
D_FF = 5632
PAST_LEN = 16384
PAGE_SIZE = 128
NORM_EPS = 1e-6
LNX_EPS = 64e-5
SUBLN_EPS = 1e-5

LANES = 128
CHUNK = 64
VMEM_LIMIT = 60 * 1024 * 1024

NT_DIMS = (((1,), (1,)), ((), ()))
TN_DIMS = (((0,), (0,)), ((), ()))


def _params(*sem):
    return pltpu.CompilerParams(dimension_semantics=sem, vmem_limit_bytes=VMEM_LIMIT)


def _dot(a, b, dims=None):
    if dims is None:
        return jnp.dot(a, b, preferred_element_type=F32)
    return lax.dot_general(a, b, dims, preferred_element_type=F32)


def _rms(x, g, eps):
    return x * lax.rsqrt(jnp.mean(x * x, axis=-1, keepdims=True) + eps) * g


def _sigmoid(x):
    return 1.0 / (1.0 + jnp.exp(-x))


def _group_matrix(scale):
    r = lax.broadcasted_iota(jnp.int32, (LANES, LANES), 0) // RWKV_N
    c = lax.broadcasted_iota(jnp.int32, (LANES, LANES), 1) // RWKV_N
    return jnp.where(r == c, scale, 0.0).astype(BF16)


def _group_sum(x, gm):
    return _dot(x.astype(BF16), gm)


FFN_TM = 1024
FFN_TF = 256


def _ffn_kernel(x_ref, g_ref, w1_ref, w3_ref, w2_ref, o_ref, h_ref):
    f = pl.program_id(1)

    @pl.when(f == 0)
    def _():
        h_ref[...] = _rms(x_ref[...], g_ref[...], NORM_EPS).astype(BF16)
        o_ref[...] = jnp.zeros_like(o_ref)

    h = h_ref[...]
    gate = _dot(h, w1_ref[...].astype(BF16))
    up = _dot(h, w3_ref[...].astype(BF16))
    act = (gate * _sigmoid(gate) * up).astype(BF16)
    o_ref[...] += _dot(act, w2_ref[...].astype(BF16))

    @pl.when(f == pl.num_programs(1) - 1)
    def _():
        o_ref[...] = x_ref[...] + 0.5 * o_ref[...]


def ffn(x, g, w13, w2, layer, half):
    m, d = x.shape
    tm = min(FFN_TM, m)
    tf = FFN_TF
    nf = D_FF // tf
    return pl.pallas_call(
        _ffn_kernel,
        grid=(m // tm, nf),
        in_specs=[
            pl.BlockSpec((tm, d), lambda i, f: (i, 0)),
            pl.BlockSpec((1, d), lambda i, f: (0, 0)),
            pl.BlockSpec((None, None, d, tf), lambda i, f: (layer, half, 0, f)),
            pl.BlockSpec((None, None, d, tf), lambda i, f: (layer, half, 0, f + nf)),
            pl.BlockSpec((None, None, tf, d), lambda i, f: (layer, half, f, 0)),
        ],
        out_specs=pl.BlockSpec((tm, d), lambda i, f: (i, 0)),
        out_shape=jax.ShapeDtypeStruct((m, d), F32),
        scratch_shapes=[pltpu.VMEM((tm, d), BF16)],
        compiler_params=_params("parallel", "arbitrary"),
        name="ffn",
    )(x, g.reshape(1, d), w13, w13, w2)


def _norm_mm_kernel(*refs, qk_epilogue, mxu_copy_scale, f32_layout):
    refs = list(refs)
    ob_ref = refs.pop() if mxu_copy_scale is not None else None
    o_ref = refs.pop() if f32_layout is not None else None
    x_ref, g_ref, w_ref = refs[:3]

    def emit(j, val):
        cols = slice(j * LANES, (j + 1) * LANES)
        if f32_layout == "rows":
            o_ref[:, cols] = val
        elif f32_layout == "transposed":
            o_ref[0, cols, :] = val.T
        elif f32_layout == "heads":
            o_ref[0, j] = val
        if ob_ref is not None:
            ob_ref[:, cols] = (val * mxu_copy_scale).astype(BF16)

    y = _dot(_rms(x_ref[...], g_ref[...], NORM_EPS).astype(BF16), w_ref[...])
    if not qk_epilogue:
        for j in range(y.shape[1] // LANES):
            emit(j, y[:, j * LANES:(j + 1) * LANES])
        return
    gh_ref, cos_ref, s1_ref, s2_ref = refs[3:7]
    gm = _group_matrix(1.0 / DIFF_DH)
    gh, cos, s1, s2 = gh_ref[...], cos_ref[...], s1_ref[...], s2_ref[...]
    half = ROT_DIM // 2
    for j in range(y.shape[1] // LANES):
        blk = y[:, j * LANES:(j + 1) * LANES]
        nb = blk * lax.rsqrt(_group_sum(blk * blk, gm) + NORM_EPS) * gh
        emit(j, nb * cos + pltpu.roll(nb, LANES - half, 1) * s1 + pltpu.roll(nb, half, 1) * s2)


def norm_mm(x, g, w, tm, rows_per_seq, rope=None, head_gain=None, mxu_copy_scale=None, f32_layout="rows"):
    m, d = x.shape
    n = w.shape[1]
    qk = rope is not None
    tiles_per_seq = rows_per_seq // tm
    seqs = m // rows_per_seq
    fixed = lambda i: (0, 0)
    in_specs = [pl.BlockSpec((tm, d), lambda i: (i, 0)), pl.BlockSpec((1, d), fixed), pl.BlockSpec((d, n), fixed)]
    args = [x, g.reshape(1, d), w]
    if qk:
        in_specs.append(pl.BlockSpec((1, LANES), fixed))
        args.append(jnp.tile(head_gain.reshape(1, DIFF_DH), (1, 2)))
        for t in rope:
            in_specs.append(pl.BlockSpec((tm, LANES), lambda i: (i % tiles_per_seq, 0)))
            args.append(t)
    out_specs, out_shape = [], []
    if f32_layout == "rows":
        out_specs.append(pl.BlockSpec((tm, n), lambda i: (i, 0)))
        out_shape.append(jax.ShapeDtypeStruct((m, n), F32))
    elif f32_layout == "transposed":
        out_specs.append(pl.BlockSpec((1, n, tm), lambda i: (i // tiles_per_seq, 0, i % tiles_per_seq)))
        out_shape.append(jax.ShapeDtypeStruct((seqs, n, rows_per_seq), F32))
    elif f32_layout == "heads":
        out_specs.append(pl.BlockSpec((1, n // LANES, tm, LANES),
                                      lambda i: (i // tiles_per_seq, 0, i % tiles_per_seq, 0)))
        out_shape.append(jax.ShapeDtypeStruct((seqs, n // LANES, rows_per_seq, LANES), F32))
    if mxu_copy_scale is not None:
        out_specs.append(pl.BlockSpec((tm, n), lambda i: (i, 0)))
        out_shape.append(jax.ShapeDtypeStruct((m, n), BF16))
    return pl.pallas_call(
        functools.partial(_norm_mm_kernel, qk_epilogue=qk, mxu_copy_scale=mxu_copy_scale, f32_layout=f32_layout),
        grid=(m // tm,),
        in_specs=in_specs,
        out_specs=out_specs,
        out_shape=out_shape,
        compiler_params=_params("parallel"),
        name="norm_mm_qk" if qk else "norm_mm",
    )(*args)


def rope_tables(pos):
    half = ROT_DIM // 2
    inv = ROPE_THETA ** (-jnp.arange(0, ROT_DIM, 2, dtype=F32) / ROT_DIM)
    ang = pos.astype(F32)[:, None] * inv[None, :]
    cos, sin = jnp.cos(ang), jnp.sin(ang)
    t = pos.shape[0]
    rest = DIFF_DH - ROT_DIM
    c64 = jnp.concatenate([cos, cos, jnp.ones((t, rest), F32)], axis=1)
    s1_64 = jnp.concatenate([-sin, jnp.zeros((t, DIFF_DH - half), F32)], axis=1)
    s2_64 = jnp.concatenate([jnp.zeros((t, half), F32), sin, jnp.zeros((t, rest), F32)], axis=1)
    return tuple(jnp.tile(z, (1, 2)) for z in (c64, s1_64, s2_64))


def _mem_kv_kernel(x_ref, g_ref, w_ref, gk_ref, k_ref, v_ref):
    h = _rms(x_ref[...], g_ref[...], NORM_EPS).astype(BF16)
    y = _dot(h, w_ref[...])
    gk = gk_ref[...]
    for j in range(MEM_HEADS):
        blk = y[:, j * MEM_DH:(j + 1) * MEM_DH]
        k_ref[:, j * MEM_DH:(j + 1) * MEM_DH] = _rms(blk, gk, NORM_EPS)
    v_ref[...] = y[:, MEM_W:]


def mem_kv(mem, g, w, gk):
    m, d = mem.shape
    return pl.pallas_call(
        _mem_kv_kernel,
        grid=(1,),
        in_specs=[
            pl.BlockSpec((m, d), lambda i: (0, 0)),
            pl.BlockSpec((1, d), lambda i: (0, 0)),
            pl.BlockSpec((d, 2 * MEM_W), lambda i: (0, 0)),
            pl.BlockSpec((1, MEM_DH), lambda i: (0, 0)),
        ],
        out_specs=[pl.BlockSpec((m, MEM_W), lambda i: (0, 0))] * 2,
        out_shape=[jax.ShapeDtypeStruct((m, MEM_W), F32)] * 2,
        compiler_params=_params("arbitrary"),
        name="mem_kv",
    )(mem, g.reshape(1, d), w, gk.reshape(1, MEM_DH))


def _mem_attn_kernel(q_ref, k_ref, v_ref, gq_ref, o_ref):
    q = q_ref[0]
    rows = q.shape[0]
    if rows < 8:
        q = jnp.broadcast_to(q, (8, q.shape[1]))
    k = k_ref[0]
    v = v_ref[0]
    gq = gq_ref[...]
    for h in range(MEM_HEADS):
        sl = slice(h * MEM_DH, (h + 1) * MEM_DH)
        qh = _rms(q[:, sl], gq, NORM_EPS).astype(BF16)
        s = _dot(qh, k[:, sl].astype(BF16), NT_DIMS) * (MEM_DH ** -0.5)
        p = jnp.exp(s - jnp.max(s, axis=-1, keepdims=True))
        o = _dot(p.astype(BF16), v[:, sl].astype(BF16)) / jnp.sum(p, axis=-1, keepdims=True)
        o_ref[0, :, sl] = o[:rows].astype(o_ref.dtype)


def mem_attn(proj, q_col_block, mk, mv, gq, tq):
    b, t, _ = proj.shape
    return pl.pallas_call(
        _mem_attn_kernel,
        grid=(b, t // tq),
        in_specs=[
            pl.BlockSpec((1, tq, MEM_W), lambda i, j: (i, j, q_col_block)),
            pl.BlockSpec((1, MEM_TOKENS, MEM_W), lambda i, j: (i, 0, 0)),
            pl.BlockSpec((1, MEM_TOKENS, MEM_W), lambda i, j: (i, 0, 0)),
            pl.BlockSpec((1, MEM_DH), lambda i, j: (0, 0)),
        ],
        out_specs=pl.BlockSpec((1, tq, MEM_W), lambda i, j: (i, j, 0)),
        out_shape=jax.ShapeDtypeStruct((b, t, MEM_W), BF16),
        compiler_params=_params("parallel", "arbitrary"),
        name="mem_attn",
    )(proj, mk, mv, gq.reshape(1, MEM_DH))


def _out_mm_kernel(x_ref, a_ref, b_ref, wa_ref, wb_ref, o_ref):
    o_ref[...] = x_ref[...] + _dot(a_ref[...], wa_ref[...]) + _dot(b_ref[...], wb_ref[...])


def out_mm(x, a, b, w, layer, tm):
    m, d = x.shape
    tn = d
    ka, kb = a.shape[1], b.shape[1]
    kb_blocks = ka // kb
    return pl.pallas_call(
        _out_mm_kernel,
        grid=(m // tm, d // tn),
        in_specs=[
            pl.BlockSpec((tm, tn), lambda i, j: (i, j)),
            pl.BlockSpec((tm, ka), lambda i, j: (i, 0)),
            pl.BlockSpec((tm, kb), lambda i, j: (i, 0)),
            pl.BlockSpec((None, ka, tn), lambda i, j: (layer, 0, j)),
            pl.BlockSpec((None, kb, tn), lambda i, j: (layer, kb_blocks, j)),
        ],
        out_specs=pl.BlockSpec((tm, tn), lambda i, j: (i, j)),
        out_shape=jax.ShapeDtypeStruct((m, d), F32),
        compiler_params=_params("parallel", "arbitrary"),
        name="out_mm",
    )(x, a, b, w, w)


def _a_prep_kernel(x_ref, xs_ref, sp_ref, g_ref, mu_ref, xn_ref, cat_ref, mix_ref, *, tiles_per_seq):
    g = g_ref[...]
    xn = _rms(x_ref[...], g, NORM_EPS)
    if tiles_per_seq is None:
        xp = sp_ref[...]
    else:
        xp = _rms(xs_ref[...], g, NORM_EPS)
        first = pl.program_id(0) % tiles_per_seq == 0
        row = lax.broadcasted_iota(jnp.int32, xn.shape, 0)
        xp = jnp.where(jnp.logical_and(row == 0, first), sp_ref[0], xp)
    xx = xp - xn
    d = xn.shape[1]
    xn_ref[...] = xn
    cat_ref[:, :d] = xn.astype(BF16)
    cat_ref[:, d:] = xx.astype(BF16)
    for i in range(3):
        mix_ref[:, i * d:(i + 1) * d] = (xn + xx * mu_ref[i:i + 1, :]).astype(BF16)


def a_prep(x, x_shift, shift_prev, g, mu, tm, rows_per_seq):
    m, d = x.shape
    if rows_per_seq == 1:
        tiles_per_seq = None
        x_shift = x
        sp, sp_spec = shift_prev, pl.BlockSpec((tm, d), lambda i: (i, 0))
    else:
        tiles_per_seq = rows_per_seq // tm
        sp = shift_prev.reshape(shift_prev.shape[0], 1, d)
        sp_spec = pl.BlockSpec((1, 1, d), lambda i: (i // tiles_per_seq, 0, 0))
    return pl.pallas_call(
        functools.partial(_a_prep_kernel, tiles_per_seq=tiles_per_seq),
        grid=(m // tm,),
        in_specs=[
            pl.BlockSpec((tm, d), lambda i: (i, 0)),
            pl.BlockSpec((tm, d), lambda i: (i, 0)),
            sp_spec,
            pl.BlockSpec((1, d), lambda i: (0, 0)),
            pl.BlockSpec((3, d), lambda i: (0, 0)),
        ],
        out_specs=[
            pl.BlockSpec((tm, d), lambda i: (i, 0)),
            pl.BlockSpec((tm, 2 * d), lambda i: (i, 0)),
            pl.BlockSpec((tm, 3 * d), lambda i: (i, 0)),
        ],
        out_shape=[
            jax.ShapeDtypeStruct((m, d), F32),
            jax.ShapeDtypeStruct((m, 2 * d), BF16),
            jax.ShapeDtypeStruct((m, 3 * d), BF16),
        ],
        compiler_params=_params("parallel"),
        name="a_prep",
    )(x, x_shift, sp, g.reshape(1, d), mu)


def _mm_kernel(a_ref, w_ref, o_ref):
    o_ref[...] = _dot(a_ref[...], w_ref[...])


def mm(a, w, layer, tm, tn=1024):
    m, k = a.shape
    n = w.shape[2]
    return pl.pallas_call(
        _mm_kernel,
        grid=(m // tm, n // tn),
        in_specs=[pl.BlockSpec((tm, k), lambda i, j: (i, 0)),
                  pl.BlockSpec((None, k, tn), lambda i, j: (layer, 0, j))],
        out_specs=pl.BlockSpec((tm, tn), lambda i, j: (i, j)),
        out_shape=jax.ShapeDtypeStruct((m, n), F32),
        compiler_params=_params("parallel", "arbitrary"),
        name="mm",
    )(a, w)


def _a_mix_kernel(k_ref, mix_ref, w1_ref, w2_ref, a1_ref, a2_ref, g1_ref, g2_ref, w0_ref, a0_ref, kk_ref,
                  ka_ref, ld_ref, kp_ref, kn_ref, b_ref, g_ref):
    d = D_MODEL
    xw, xa, xg = mix_ref[:, :d], mix_ref[:, d:2 * d], mix_ref[:, 2 * d:]
    wl = _dot(jnp.tanh(_dot(xw, w1_ref[...])).astype(BF16), w2_ref[...]) + w0_ref[...]
    w = -(jnp.maximum(-wl, 0.0) + jnp.log(1.0 + jnp.exp(-jnp.abs(wl)))) - 0.5
    ld_ref[...] = -jnp.exp(w)
    a = _sigmoid(_dot(_dot(xa, a1_ref[...]).astype(BF16), a2_ref[...]) + a0_ref[...])
    g_ref[...] = _dot(_sigmoid(_dot(xg, g1_ref[...])).astype(BF16), g2_ref[...])
    k = k_ref[...]
    kp_ref[...] = k * (1.0 + (a - 1.0) * ka_ref[...])
    kraw = k * kk_ref[...]
    gm = _group_matrix(1.0)
    for j in range(RWKV_W // LANES):
        sl = slice(j * LANES, (j + 1) * LANES)
        blk = kraw[:, sl]
        kn = blk / jnp.maximum(jnp.sqrt(_group_sum(blk * blk, gm)), 1e-12)
        kn_ref[:, sl] = kn
        b_ref[:, sl] = kn * a[:, sl]


def a_mix(proj, mix, lw, tm):
    m = proj.shape[0]
    row = lambda i: (i, 0)
    fixed = lambda i: (0, 0)
    vec = pl.BlockSpec((1, RWKV_W), fixed)
    out = jax.ShapeDtypeStruct((m, RWKV_W), F32)
    return pl.pallas_call(
        _a_mix_kernel,
        grid=(m // tm,),
        in_specs=[
            pl.BlockSpec((tm, RWKV_W), lambda i: (i, 1)),
            pl.BlockSpec((tm, 3 * D_MODEL), row),
            pl.BlockSpec((D_MODEL, LORA_PAD), fixed), pl.BlockSpec((LORA_PAD, RWKV_W), fixed),
            pl.BlockSpec((D_MODEL, LORA_PAD), fixed), pl.BlockSpec((LORA_PAD, RWKV_W), fixed),
            pl.BlockSpec((D_MODEL, LORA_G), fixed), pl.BlockSpec((LORA_G, RWKV_W), fixed),
            vec, vec, vec, vec,
        ],
        out_specs=[pl.BlockSpec((tm, RWKV_W), row)] * 5,
        out_shape=[out] * 5,
        compiler_params=_params("parallel"),
        name="a_mix",
    )(proj, mix, lw["w1"], lw["w2"], lw["a1"], lw["a2"], lw["g1"], lw["g2"],
      lw["w0"], lw["a0"], lw["k_k"], lw["k_a"])


def _wkv_masks(c):
    c2 = 2 * c
    ri = lax.broadcasted_iota(jnp.int32, (c2, c2), 0)
    ci = lax.broadcasted_iota(jnp.int32, (c2, c2), 1)
    same = (ri // c) == (ci // c)
    masks = dict(
        in_head0=lax.broadcasted_iota(jnp.int32, (c, LANES), 1) < RWKV_N,
        tri=jnp.where(lax.broadcasted_iota(jnp.int32, (c, c), 0) >= lax.broadcasted_iota(jnp.int32, (c, c), 1),
                      1.0, 0.0).astype(BF16),
        strict=jnp.logical_and(same, (ri % c) > (ci % c)),
        incl=jnp.logical_and(same, (ri % c) >= (ci % c)),
        eye=jnp.where(ri == ci, 1.0, 0.0),
        diag8=(ri // 8) == (ci // 8),
        lower_left=[],
    )
    size = 8
    while size < c:
        masks["lower_left"].append(jnp.logical_and(
            (ri // (2 * size)) == (ci // (2 * size)),
            jnp.logical_and((ri // size) % 2 == 1, (ci // size) % 2 == 0)))
        size *= 2
    return masks


def _each(f, *lists):
    return [f(*args) for args in zip(*lists)]


def _wkv_chunk(s, r, ld, k, v, kn, b, mk):
    c = r[0].shape[0]
    c2 = 2 * c
    in_head0 = mk["in_head0"]
    tri = mk["tri"]

    def stack(z):
        return jnp.concatenate([jnp.where(in_head0, z, 0.0), jnp.where(in_head0, 0.0, z)], axis=0)

    def bdot(x, y):
        return _dot(x.astype(BF16), y.astype(BF16))

    p1 = _each(lambda z: z.astype(BF16), ld)
    rem = _each(lambda z, p: z - p.astype(F32), ld, p1)
    p2 = _each(lambda z: z.astype(BF16), rem)
    p3 = _each(lambda z, p: (z - p.astype(F32)).astype(BF16), rem, p2)
    cum = _each(lambda a1, a2, a3: _dot(tri, a1) + _dot(tri, a2) + _dot(tri, a3), p1, p2, p3)
    eg = _each(jnp.exp, cum)
    einv = _each(lambda z: jnp.exp(-z), cum)
    at_s = _each(lambda n, z, d: stack(-n * jnp.exp(z - d)).astype(BF16), kn, cum, ld)
    rt_s = _each(lambda x, e: stack(x * e).astype(BF16), r, eg)
    bt = _each(lambda x, e: x * e, b, einv)
    kt = _each(lambda x, e: x * e, k, einv)
    v_s = _each(lambda x: stack(x).astype(BF16), v)

    gmat = _each(lambda a, x, y, z: _dot(jnp.concatenate([a, x], axis=0),
                                         jnp.concatenate([y, y, z, z], axis=0).astype(BF16), NT_DIMS),
                 at_s, rt_s, bt, kt)
    a_ab = _each(lambda g: jnp.where(mk["strict"], g[:c2, :c2], 0.0), gmat)
    a_ak = _each(lambda g: jnp.where(mk["strict"], g[:c2, c2:], 0.0).astype(BF16), gmat)
    a_rr = _each(lambda g: jnp.concatenate([jnp.where(mk["incl"], g[c2:, :c2], 0.0),
                                            jnp.where(mk["incl"], g[c2:, c2:], 0.0)], axis=1).astype(BF16), gmat)

    ad = _each(lambda a: jnp.where(mk["diag8"], a, 0.0), a_ab)
    ad2 = _each(bdot, ad, ad)
    ad4 = _each(bdot, ad2, ad2)
    inv = _each(lambda a: mk["eye"] + a, ad)
    inv = _each(lambda x, y: x + bdot(x, y), inv, ad2)
    inv = _each(lambda x, y: x + bdot(x, y), inv, ad4)
    for lower_left in mk["lower_left"]:
        half = _each(lambda x, a: bdot(x, jnp.where(lower_left, a, 0.0)), inv, a_ab)
        inv = _each(lambda x, y: x + bdot(y, x), inv, half)

    s_b = _each(lambda z: z.astype(BF16), s)
    rhs = _each(lambda a, x, y, z: _dot(a, x) + _dot(y, z, NT_DIMS), a_ak, v_s, at_s, s_b)
    u = _each(lambda x, y: bdot(x, y).astype(BF16), inv, rhs)
    uv = _each(lambda x, y: jnp.concatenate([x, y], axis=0), u, v_s)
    y_s = _each(lambda x, z, a, w: _dot(x, z, NT_DIMS) + _dot(a, w), rt_s, s_b, a_rr, uv)
    ds = _each(lambda w, x, y: _dot(w, jnp.concatenate([stack(x), stack(y)], axis=0).astype(BF16), TN_DIMS),
               uv, bt, kt)
    y = _each(lambda z: z[:c] + z[c:], y_s)
    s_new = _each(lambda z, dz, e: (z + dz) * e[c - 1:c, :], s, ds, eg)
    return y, s_new


def _wkv_kernel(r_ref, ld_ref, k_ref, v_ref, kn_ref, b_ref, y_ref, s_ref, st_ref, *, chunks, pairs):
    @pl.when(pl.program_id(2) == 0)
    def _():
        st_ref[...] = jnp.zeros_like(st_ref)

    mk = _wkv_masks(CHUNK)
    s = [st_ref[p] for p in range(pairs)]
    for i in range(chunks):
        rows = slice(i * CHUNK, (i + 1) * CHUNK)
        cut = lambda ref: [ref[0, rows, p * LANES:(p + 1) * LANES] for p in range(pairs)]
        y, s = _wkv_chunk(s, cut(r_ref), cut(ld_ref), cut(k_ref), cut(v_ref), cut(kn_ref), cut(b_ref), mk)
        for p in range(pairs):
            y_ref[0, rows, p * LANES:(p + 1) * LANES] = y[p]
    for p in range(pairs):
        st_ref[p] = s[p]

    @pl.when(pl.program_id(2) == pl.num_programs(2) - 1)
    def _():
        s_ref[0] = st_ref[...]


WKV_PAIRS = 12
WKV_CHUNKS = 2


def wkv_scan(proj, ld, kp, kn, b):
    bsz, t, _ = proj.shape
    groups = RWKV_W // LANES // WKV_PAIRS
    tb = WKV_CHUNKS * CHUNK
    width = WKV_PAIRS * LANES
    blk = lambda off: pl.BlockSpec((1, tb, width), lambda i, p, c: (i, c, p + off))
    return pl.pallas_call(
        functools.partial(_wkv_kernel, chunks=WKV_CHUNKS, pairs=WKV_PAIRS),
        grid=(bsz, groups, t // tb),
        in_specs=[blk(0), blk(0), blk(0), blk(2 * groups), blk(0), blk(0)],
        out_specs=[
            pl.BlockSpec((1, tb, width), lambda i, p, c: (i, c, p)),
            pl.BlockSpec((1, WKV_PAIRS, LANES, LANES), lambda i, p, c: (i, p, 0, 0)),
        ],
        out_shape=[
            jax.ShapeDtypeStruct((bsz, t, RWKV_W), F32),
            jax.ShapeDtypeStruct((bsz, RWKV_W // LANES, LANES, LANES), F32),
        ],
        scratch_shapes=[pltpu.VMEM((WKV_PAIRS, LANES, LANES), F32)],
        compiler_params=_params("parallel", "parallel", "arbitrary"),
        name="wkv_scan",
    )(proj, ld, kp, proj, kn, b)


def _wkv_step_kernel(s_ref, r_ref, ld_ref, k_ref, v_ref, kn_ref, b_ref, y_ref, so_ref):
    s = s_ref[0]
    sa = jnp.sum(s * (-kn_ref[0]), axis=-1, keepdims=True)
    s = s * jnp.exp(ld_ref[0]) + sa * b_ref[0] + v_ref[0] * k_ref[0]
    so_ref[0] = s
    y_ref[0] = jnp.sum(s * r_ref[0], axis=-1, keepdims=True)


def wkv_step(s0, r, ld, k, v, kn, b):
    bsz = s0.shape[0]
    h, n = RWKV_HEADS, RWKV_N
    st = pl.BlockSpec((1, h, n, n), lambda i: (i, 0, 0, 0))
    rw = pl.BlockSpec((1, h, 1, n), lambda i: (i, 0, 0, 0))
    cl = pl.BlockSpec((1, h, n, 1), lambda i: (i, 0, 0, 0))
    return pl.pallas_call(
        _wkv_step_kernel,
        grid=(bsz,),
        in_specs=[st, rw, rw, rw, cl, rw, rw],
        out_specs=[cl, st],
        out_shape=[jax.ShapeDtypeStruct((bsz, h, n, 1), F32), jax.ShapeDtypeStruct((bsz, h, n, n), F32)],
        compiler_params=_params("parallel"),
        name="wkv_step",
    )(s0, r, ld, k, v, kn, b)


def _a_post_kernel(y_ref, r_ref, kp_ref, v_ref, g_ref, lw_ref, lb_ref, rk_ref, o_ref):
    gsum = _group_matrix(1.0)
    gmean = _group_matrix(1.0 / RWKV_N)
    for j in range(RWKV_W // LANES):
        sl = slice(j * LANES, (j + 1) * LANES)
        y = y_ref[:, sl]
        cen = y - _group_sum(y, gmean)
        yn = cen * lax.rsqrt(_group_sum(cen * cen, gmean) + LNX_EPS) * lw_ref[:, sl] + lb_ref[:, sl]
        bonus = _group_sum(r_ref[:, sl] * kp_ref[:, sl] * rk_ref[:, sl], gsum) * v_ref[:, sl]
        o_ref[:, sl] = ((yn + bonus) * g_ref[:, sl]).astype(o_ref.dtype)


def a_post(y, proj, kp, g, lnx_w, lnx_b, r_k, tm):
    m = y.shape[0]
    row = lambda i: (i, 0)
    vec = pl.BlockSpec((1, RWKV_W), lambda i: (0, 0))
    tile = pl.BlockSpec((tm, RWKV_W), row)
    return pl.pallas_call(
        _a_post_kernel,
        grid=(m // tm,),
        in_specs=[tile, pl.BlockSpec((tm, RWKV_W), lambda i: (i, 0)), tile,
                  pl.BlockSpec((tm, RWKV_W), lambda i: (i, 2)), tile, vec, vec, vec],
        out_specs=tile,
        out_shape=jax.ShapeDtypeStruct((m, RWKV_W), BF16),
        compiler_params=_params("parallel"),
        name="a_post",
    )(y, proj, kp, proj, g, lnx_w.reshape(1, RWKV_W), lnx_b.reshape(1, RWKV_W), r_k.reshape(1, RWKV_W))


def _lambda(lam_ref, lam_init):
    lq = lam_ref[...]
    l1 = jnp.sum(lq[0:1] * lq[1:2], axis=-1, keepdims=True)
    l2 = jnp.sum(lq[2:3] * lq[3:4], axis=-1, keepdims=True)
    return jnp.exp(l1) - jnp.exp(l2) + lam_init


ATTN_HEADS = 2
ATTN_TQ = 512
LOG2E = 1.4426950408889634
Q_SCALE = DIFF_DH ** -0.5 * LOG2E


def _diff_attn_kernel(qi_ref, ki_ref, q_ref, k_ref, v_ref, lam_ref, gs_ref, o_ref, m_ref, l_ref, acc_ref, *,
                      tq, lam_init):
    qi = qi_ref[pl.program_id(2)]
    ki = ki_ref[pl.program_id(2)]

    @pl.when(ki == 0)
    def _():
        m_ref[...] = jnp.full_like(m_ref, -jnp.inf)
        l_ref[...] = jnp.zeros_like(l_ref)
        acc_ref[...] = jnp.zeros_like(acc_ref)

    def step(diagonal):
        lane = lax.broadcasted_iota(jnp.int32, (tq, DIFF_DV), 1)
        if diagonal:
            visible = (lax.broadcasted_iota(jnp.int32, (tq, tq), 1)
                       <= lax.broadcasted_iota(jnp.int32, (tq, tq), 0))
        for h in range(ATTN_HEADS):
            cols = slice(h * DIFF_DV, (h + 1) * DIFF_DV)
            q = q_ref[0, :, cols]
            kb = k_ref[0, :, cols]
            v1 = jnp.concatenate([v_ref[0, :, cols], jnp.ones((tq, LANES), BF16)], axis=1)
            for c in range(2):
                idx = 2 * h + c
                in_comp = (lane < DIFF_DH) if c == 0 else (lane >= DIFF_DH)
                s = _dot(jnp.where(in_comp, q, jnp.zeros_like(q)), kb, NT_DIMS)
                if diagonal:
                    s = jnp.where(visible, s, -jnp.inf)
                m_old = m_ref[idx]
                m_new = jnp.maximum(m_old, jnp.max(s, axis=-1, keepdims=True))
                alpha = jnp.exp2(m_old - m_new)
                p = jnp.exp2(s - jnp.concatenate([m_new] * (tq // LANES), axis=1))
                pv = _dot(p.astype(BF16), v1)
                l_ref[idx] = alpha * l_ref[idx] + pv[:, DIFF_DV:]
                acc_ref[idx] = alpha * acc_ref[idx] + pv[:, :DIFF_DV]
                m_ref[idx] = m_new

    @pl.when(ki < qi)
    def _():
        step(False)

    @pl.when(ki == qi)
    def _():
        step(True)
        lam = _lambda(lam_ref, lam_init)
        for h in range(ATTN_HEADS):
            o = acc_ref[2 * h] / l_ref[2 * h] - lam * (acc_ref[2 * h + 1] / l_ref[2 * h + 1])
            o_ref[0, :, h * DIFF_DV:(h + 1) * DIFF_DV] = (
                _rms(o, gs_ref[...], SUBLN_EPS) * (1.0 - lam_init)).astype(o_ref.dtype)


def diff_attn_prompt(q, k, v, lam_p, subln, lam_init):
    b, t, _ = q.shape
    tq = ATTN_TQ
    width = ATTN_HEADS * DIFF_DV
    chains = 2 * ATTN_HEADS
    blocks = t // tq
    pairs = [(qi, ki) for qi in range(blocks) for ki in range(qi + 1)]
    q_of = jnp.asarray([p[0] for p in pairs], jnp.int32)
    k_of = jnp.asarray([p[1] for p in pairs], jnp.int32)
    q_map = lambda i, h, p, q_of, k_of: (i, q_of[p], h)
    kv_map = lambda i, h, p, q_of, k_of: (i, k_of[p], h)
    fixed = lambda i, h, p, q_of, k_of: (0, 0)
    grid_spec = pltpu.PrefetchScalarGridSpec(
        num_scalar_prefetch=2,
        grid=(b, DIFF_HEADS // ATTN_HEADS, len(pairs)),
        in_specs=[
            pl.BlockSpec((1, tq, width), q_map),
            pl.BlockSpec((1, tq, width), kv_map),
            pl.BlockSpec((1, tq, width), kv_map),
            pl.BlockSpec((4, DIFF_DH), fixed),
            pl.BlockSpec((1, DIFF_DV), fixed),
        ],
        out_specs=pl.BlockSpec((1, tq, width), q_map),
        scratch_shapes=[pltpu.VMEM((chains, tq, LANES), F32), pltpu.VMEM((chains, tq, LANES), F32),
                        pltpu.VMEM((chains, tq, DIFF_DV), F32)],
    )
    return pl.pallas_call(
        functools.partial(_diff_attn_kernel, tq=tq, lam_init=lam_init),
        grid_spec=grid_spec,
        out_shape=jax.ShapeDtypeStruct((b, t, DIFF_W), BF16),
        compiler_params=_params("parallel", "parallel", "arbitrary"),
        name="diff_attn_prompt",
    )(q_of, k_of, q, k, v, lam_p, subln.reshape(1, DIFF_DV))


DEC_PAGES = 4
DEC_ROWS = 8


def _dec_attn_kernel(pt_ref, q_ref, kn_ref, vn_ref, lam_ref, gs_ref, *refs, lam_init):
    k_refs = refs[:DEC_PAGES]
    v_refs = refs[DEC_PAGES:2 * DEC_PAGES]
    o_ref, qb_ref, m_ref, l_ref, acc_ref = refs[2 * DEC_PAGES:]
    step = pl.program_id(1)
    row = lax.broadcasted_iota(jnp.int32, (DIFF_HEADS, DEC_ROWS, LANES), 1)

    def comp_rows(prod):
        s0 = jnp.sum(prod[:, :DIFF_DH, :], axis=1, keepdims=True)
        s1 = jnp.sum(prod[:, DIFF_DH:, :], axis=1, keepdims=True)
        return jnp.where(row == 0, s0, jnp.where(row == 1, s1, 0.0))

    @pl.when(step == 0)
    def _():
        q_col = q_ref[0] * (DIFF_DH ** -0.5)
        qb_ref[...] = jnp.broadcast_to(q_col, qb_ref.shape)
        m_ref[...] = comp_rows(q_col * kn_ref[0])
        l_ref[...] = jnp.ones_like(l_ref)
        acc_ref[...] = jnp.broadcast_to(vn_ref[0], acc_ref.shape)

    qb = qb_ref[...]
    for k_ref, v_ref in zip(k_refs, v_refs):
        s = comp_rows(k_ref[0] * qb)
        m_old = m_ref[...]
        m_new = jnp.maximum(m_old, jnp.max(s, axis=-1, keepdims=True))
        alpha = jnp.exp(m_old - m_new)
        p = jnp.exp(s - m_new)
        l_ref[...] = alpha * l_ref[...] + jnp.sum(p, axis=-1, keepdims=True)
        pv = lax.dot_general(p.astype(BF16), v_ref[0].astype(BF16), (((2,), (1,)), ((0,), (0,))),
                             preferred_element_type=F32)
        acc_ref[...] = alpha * acc_ref[...] + pv
        m_ref[...] = m_new

    @pl.when(step == pl.num_programs(1) - 1)
    def _():
        lam = _lambda(lam_ref, lam_init)
        w = acc_ref[...] / l_ref[...]
        o = w[:, 0:1, :] - lam * w[:, 1:2, :]
        o = o * lax.rsqrt(jnp.mean(o * o, axis=-1, keepdims=True) + SUBLN_EPS) * gs_ref[...] * (1.0 - lam_init)
        o_ref[0] = jnp.broadcast_to(o, o_ref.shape[1:])


def diff_attn_decode(q_col, k_col, v_new, cache_kt, cache_v, page_table, lam_p, subln, lam_init):
    b = q_col.shape[0]
    n_pages = page_table.shape[1]
    col = pl.BlockSpec((1, DIFF_HEADS, DIFF_DV, 1), lambda i, s, pt: (i, 0, 0, 0))
    state = pltpu.VMEM((DIFF_HEADS, DEC_ROWS, LANES), F32)

    def page_spec(j):
        return pl.BlockSpec((1, DIFF_HEADS, PAGE_SIZE, DIFF_DV), lambda i, s, pt: (pt[i, s * DEC_PAGES + j], 0, 0, 0))

    grid_spec = pltpu.PrefetchScalarGridSpec(
        num_scalar_prefetch=1,
        grid=(b, n_pages // DEC_PAGES),
        in_specs=[col, col,
                  pl.BlockSpec((1, DIFF_HEADS, 1, DIFF_DV), lambda i, s, pt: (i, 0, 0, 0)),
                  pl.BlockSpec((4, DIFF_DH), lambda i, s, pt: (0, 0)),
                  pl.BlockSpec((1, DIFF_DV), lambda i, s, pt: (0, 0))]
        + [page_spec(j) for j in range(DEC_PAGES)] * 2,
        out_specs=pl.BlockSpec((1, DIFF_HEADS, DEC_ROWS, DIFF_DV), lambda i, s, pt: (i, 0, 0, 0)),
        scratch_shapes=[pltpu.VMEM((DIFF_HEADS, DIFF_DV, PAGE_SIZE), F32), state, state, state],
    )
    return pl.pallas_call(
        functools.partial(_dec_attn_kernel, lam_init=lam_init),
        grid_spec=grid_spec,
        out_shape=jax.ShapeDtypeStruct((b, DIFF_HEADS, DEC_ROWS, DIFF_DV), F32),
        compiler_params=_params("parallel", "arbitrary"),
        name="diff_attn_decode",
    )(page_table, q_col, k_col, v_new, lam_p, subln.reshape(1, DIFF_DV),
      *([cache_kt] * DEC_PAGES), *([cache_v] * DEC_PAGES))


def _pad_lora(w_in, w_out):
    pad = LORA_PAD - w_in.shape[1]
    return (jnp.pad(w_in, ((0, 0), (0, pad))).astype(BF16), jnp.pad(w_out, ((0, pad), (0, 0))).astype(BF16))


def kernel(x_prompt, x_sample, mem_prompt, state_wkv, state_shift, cache_mem_k, cache_mem_v, cache_k, cache_v, page_table, ffn_norm, ffn_w13, ffn_w2, mix_norm, w_out, mem_norm, mem_w_kv, mem_q_norm, mem_k_norm, a_w_in, a_mu, a_w0, a_w1, a_w2, a_a0, a_a1, a_a2, a_g1, a_g2, a_k_k, a_k_a, a_r_k, a_lnx_w, a_lnx_b, kv_norm, kv_w, k_norm, b_w_in, b_q_norm, b_lam, b_subln):
    d = D_MODEL
    wout_b = w_out.astype(BF16)
    memw_b = mem_w_kv.astype(BF16)
    awin_b = a_w_in.astype(BF16)
    kvw_k, kvw_v = kv_w[:, :DIFF_W].astype(BF16), kv_w[:, DIFF_W:].astype(BF16)
    bq_b, bm_b = b_w_in[:, :, :DIFF_W].astype(BF16), b_w_in[:, :, DIFF_W:].astype(BF16)
    loras = []
    for i in range(N_A):
        w1, w2 = _pad_lora(a_w1[i], a_w2[i])
        a1, a2 = _pad_lora(a_a1[i], a_a2[i])
        loras.append(dict(w1=w1, w2=w2, a1=a1, a2=a2, g1=a_g1[i].astype(BF16), g2=a_g2[i].astype(BF16),
                          w0=a_w0[i].reshape(1, RWKV_W), a0=a_a0[i].reshape(1, RWKV_W),
                          k_k=a_k_k[i].reshape(1, RWKV_W), k_a=a_k_a[i].reshape(1, RWKV_W)))

    def run(x3, shift_prev, wkv0, mk, mv, pos, decode):
        bsz, t, _ = x3.shape
        m = bsz * t
        tm = min(512, m)
        x = x3.reshape(m, d)
        rope = rope_tables(jnp.broadcast_to(pos, (m,)) if decode else pos)
        rope_rows = m if decode else t
        shifts, states = [], []
        for i in range(N_A):
            x = ffn(x, ffn_norm[i, 0], ffn_w13, ffn_w2, i, 0)
            x_shift = jnp.concatenate([jnp.zeros((bsz, 1, d), F32), x.reshape(bsz, t, d)[:, :-1]], axis=1)
            xn, cat, mix = a_prep(x, x_shift.reshape(m, d), shift_prev[i], mix_norm[i], a_mu[i], tm, t)
            shifts.append(xn.reshape(bsz, t, d)[:, -1])
            proj = mm(cat, awin_b, i, tm)
            ld, kp, kn, bvec, gate = a_mix(proj, mix, loras[i], min(256, m))
            proj3 = proj.reshape(bsz, t, A_IN)
            if decode:
                heads = lambda z: z.reshape(bsz, RWKV_HEADS, 1, RWKV_N)
                y, s_new = wkv_step(
                    wkv0[i], heads(proj[:, :RWKV_W]), heads(ld), heads(kp),
                    proj[:, 2 * RWKV_W:3 * RWKV_W].reshape(bsz, RWKV_HEADS, RWKV_N, 1), heads(kn), heads(bvec))
                y = y.reshape(m, RWKV_W)
            else:
                r3 = lambda z: z.reshape(bsz, t, RWKV_W)
                y, s_pairs = wkv_scan(proj3, r3(ld), r3(kp), r3(kn), r3(bvec))
                y = y.reshape(m, RWKV_W)
                s_new = jnp.stack([s_pairs[:, :, :RWKV_N, :RWKV_N], s_pairs[:, :, RWKV_N:, RWKV_N:]], axis=2)
                s_new = s_new.reshape(bsz, RWKV_HEADS, RWKV_N, RWKV_N)
            states.append(s_new)
            y_mix = a_post(y, proj, kp, gate, a_lnx_w[i], a_lnx_b[i], a_r_k[i], min(256, m))
            o_mem = mem_attn(proj3, 3 * RWKV_W // MEM_W, mk[i], mv[i], mem_q_norm[i], min(512, t))
            x = out_mm(x, y_mix, o_mem.reshape(m, MEM_W), wout_b, i, tm)
            x = ffn(x, ffn_norm[i, 1], ffn_w13, ffn_w2, i, 1)
        if decode:
            k_rows, = norm_mm(x, kv_norm, kvw_k, tm, rope_rows, rope, k_norm)
            v_rows, = norm_mm(x, kv_norm, kvw_v, tm, rope_rows)
            k_col = k_rows.reshape(bsz, DIFF_HEADS, DIFF_DV, 1)
            v_row = v_rows.reshape(bsz, DIFF_HEADS, 1, DIFF_DV)
            k_out = k_rows.reshape(bsz, t, DIFF_HEADS, 2, DIFF_DH)
            v_out = v_rows.reshape(bsz, t, DIFF_HEADS, DIFF_DV)
        else:
            k_t, k_b = norm_mm(x, kv_norm, kvw_k, tm, rope_rows, rope, k_norm, mxu_copy_scale=1.0,
                               f32_layout="transposed")
            v_h, v_b = norm_mm(x, kv_norm, kvw_v, tm, rope_rows, mxu_copy_scale=1.0, f32_layout="heads")
            k_b, v_b = k_b.reshape(bsz, t, DIFF_W), v_b.reshape(bsz, t, DIFF_W)
            k_out = jnp.transpose(k_t.reshape(bsz, DIFF_HEADS, 2, DIFF_DH, t), (0, 4, 1, 2, 3))
            v_out = jnp.transpose(v_h, (0, 2, 1, 3))
        for j in range(N_B):
            i = N_A + j
            lam_init = 0.8 - 0.6 * math.exp(-0.3 * i)
            x = ffn(x, ffn_norm[i, 0], ffn_w13, ffn_w2, i, 0)
            q_mem = norm_mm(x, mix_norm[i], bm_b[j], tm, rope_rows)[0].reshape(bsz, t, MEM_W)
            if decode:
                q, = norm_mm(x, mix_norm[i], bq_b[j], tm, rope_rows, rope, b_q_norm[j])
                o = diff_attn_decode(q.reshape(bsz, DIFF_HEADS, DIFF_DV, 1), k_col, v_row, cache_kt, cache_vt,
                                     page_table, b_lam[j], b_subln[j], lam_init)
                o = o[:, :, 0, :].astype(BF16)
            else:
                q_b, = norm_mm(x, mix_norm[i], bq_b[j], tm, rope_rows, rope, b_q_norm[j], mxu_copy_scale=Q_SCALE,
                               f32_layout=None)
                o = diff_attn_prompt(q_b.reshape(bsz, t, DIFF_W), k_b, v_b, b_lam[j], b_subln[j], lam_init)
            o_mem = mem_attn(q_mem, 0, mk[i], mv[i], mem_q_norm[i], min(512, t))
            x = out_mm(x, o.reshape(m, DIFF_W), o_mem.reshape(m, MEM_W), wout_b, i, tm)
            x = ffn(x, ffn_norm[i, 1], ffn_w13, ffn_w2, i, 1)
        return x.reshape(bsz, t, d), jnp.stack(states), jnp.stack(shifts), k_out, v_out

    n_pool = cache_k.shape[0]
    cache_kt = jnp.transpose(cache_k, (0, 2, 3, 4, 1)).reshape(n_pool, DIFF_HEADS, DIFF_DV, PAGE_SIZE)
    cache_vt = jnp.transpose(cache_v, (0, 2, 1, 3))

    bp, tp, _ = x_prompt.shape
    bs, ts, _ = x_sample.shape
    assert ts == 1, "the sample group is decoded one token per sequence"
    mem2 = mem_prompt.reshape(bp * MEM_TOKENS, d)
    mk_p, mv_p = [], []
    for i in range(DEPTH):
        mk_i, mv_i = mem_kv(mem2, mem_norm[i], memw_b[i], mem_k_norm[i])
        mk_p.append(mk_i.reshape(bp, MEM_TOKENS, MEM_W))
        mv_p.append(mv_i.reshape(bp, MEM_TOKENS, MEM_W))

    y_p, wkv_p, shift_p, k_p, v_p = run(
        x_prompt, jnp.zeros((N_A, bp, d), F32), None, mk_p, mv_p, jnp.arange(tp), decode=False)
    mk_s = cache_mem_k.reshape(DEPTH, bs, MEM_TOKENS, MEM_W)
    mv_s = cache_mem_v.reshape(DEPTH, bs, MEM_TOKENS, MEM_W)
    y_s, wkv_s, shift_s, k_s, v_s = run(
        x_sample, state_shift, state_wkv, mk_s, mv_s, jnp.full((1,), PAST_LEN, jnp.int32), decode=True)

    memshape = lambda zs: jnp.stack(zs).reshape(DEPTH, bp, MEM_TOKENS, MEM_HEADS, MEM_DH)
    return (y_p, y_s, wkv_p, shift_p, wkv_s, shift_s, k_p, v_p, k_s, v_s, memshape(mk_p), memshape(mv_p))
```

```python
import functools
import math

import jax
import jax.numpy as jnp
from jax import lax
from jax.experimental import pallas as pl
from jax.experimental.pallas import tpu as pltpu

F32 = jnp.float32
BF16 = jnp.bfloat16

D_MODEL = 2048
DEPTH = 4
N_A = 2
N_B = 2
MEM_TOKENS = 256
MEM_HEADS = 4
MEM_W = 512
MEM_DH = 128
RWKV_W = 1536
RWKV_N = 64
RWKV_HEADS = 24
LORA_PAD = 128
LORA_G = 256
A_IN = 3 * RWKV_W + MEM_W
DIFF_W = 1536
DIFF_DV = 128
DIFF_HEADS = 12
DIFF_DH = 64
ROT_DIM = 16
ROPE_THETA = 500000.0
D_FF = 5632
PAST_LEN = 16384
PAGE_SIZE = 128
NORM_EPS = 1e-6
LNX_EPS = 64e-5
SUBLN_EPS = 1e-5

LANES = 128
CHUNK = 64
VMEM_LIMIT = 60 * 1024 * 1024

NT_DIMS = (((1,), (1,)), ((), ()))
TN_DIMS = (((0,), (0,)), ((), ()))


def _params(*sem):
    return pltpu.CompilerParams(dimension_semantics=sem, vmem_limit_bytes=VMEM_LIMIT)


def _dot(a, b, dims=None):
    if dims is None:
        return jnp.dot(a, b, preferred_element_type=F32)
    return lax.dot_general(a, b, dims, preferred_element_type=F32)


def _rms(x, g, eps):
    return x * lax.rsqrt(jnp.mean(x * x, axis=-1, keepdims=True) + eps) * g


def _sigmoid(x):
    return 1.0 / (1.0 + jnp.exp(-x))


def _group_matrix(scale):
    r = lax.broadcasted_iota(jnp.int32, (LANES, LANES), 0) // RWKV_N
    c = lax.broadcasted_iota(jnp.int32, (LANES, LANES), 1) // RWKV_N
    return jnp.where(r == c, scale, 0.0).astype(BF16)


def _group_sum(x, gm):
    return _dot(x.astype(BF16), gm)


FFN_TM = 1024
FFN_TF = 256


FFN_RIDERS = 16


def _ffn_kernel(x_ref, xs_ref, g_ref, w1_ref, w3_ref, w2_ref, o_ref, os_ref, h_ref):
    i = pl.program_id(0)
    f = pl.program_id(1)
    tm = x_ref.shape[0]
    ns = xs_ref.shape[0]

    @pl.when(f == 0)
    def _():
        h_ref[:tm, :] = _rms(x_ref[...], g_ref[...], NORM_EPS).astype(BF16)
        o_ref[...] = x_ref[...]

    @pl.when(jnp.logical_and(f == 0, i == 0))
    def _():
        h_ref[tm:, :] = jnp.zeros((FFN_RIDERS, h_ref.shape[1]), BF16)
        h_ref[tm:tm + ns, :] = _rms(xs_ref[...], g_ref[...], NORM_EPS).astype(BF16)
        os_ref[...] = xs_ref[...]

    def half_swiglu(h):
        gate = _dot(h, w1_ref[...].astype(BF16))
        up = _dot(h, w3_ref[...].astype(BF16))
        act = (0.5 * (gate * _sigmoid(gate) * up)).astype(BF16)
        return _dot(act, w2_ref[...].astype(BF16))

    @pl.when(i == 0)
    def _():
        y = half_swiglu(h_ref[...])
        o_ref[...] += y[:tm]
        os_ref[...] += y[tm:tm + ns]

    @pl.when(i > 0)
    def _():
        o_ref[...] += half_swiglu(h_ref[:tm, :])


def ffn(x, xs, g, w13, w2, layer, half):
    m, d = x.shape
    ns = xs.shape[0]
    assert ns <= FFN_RIDERS
    tm = min(FFN_TM, m)
    tf = FFN_TF
    nf = D_FF // tf
    return pl.pallas_call(
        _ffn_kernel,
        grid=(m // tm, nf),
        in_specs=[
            pl.BlockSpec((tm, d), lambda i, f: (i, 0)),
            pl.BlockSpec((ns, d), lambda i, f: (0, 0)),
            pl.BlockSpec((1, d), lambda i, f: (0, 0)),
            pl.BlockSpec((None, None, d, tf), lambda i, f: (layer, half, 0, f)),
            pl.BlockSpec((None, None, d, tf), lambda i, f: (layer, half, 0, f + nf)),
            pl.BlockSpec((None, None, tf, d), lambda i, f: (layer, half, f, 0)),
        ],
        out_specs=[pl.BlockSpec((tm, d), lambda i, f: (i, 0)), pl.BlockSpec((ns, d), lambda i, f: (0, 0))],
        out_shape=[jax.ShapeDtypeStruct((m, d), F32), jax.ShapeDtypeStruct((ns, d), F32)],
        scratch_shapes=[pltpu.VMEM((tm + FFN_RIDERS, d), BF16)],
        compiler_params=_params("arbitrary", "arbitrary"),
        name="ffn",
    )(x, xs, g.reshape(1, d), w13, w13, w2)


def _norm_mm_kernel(*refs, qk_epilogue, mxu_copy_scale, f32_layout):
    refs = list(refs)
    ob_ref = refs.pop() if mxu_copy_scale is not None else None
    o_ref = refs.pop() if f32_layout is not None else None
    x_ref, g_ref, w_ref = refs[:3]

    def emit(j, val):
        cols = slice(j * LANES, (j + 1) * LANES)
        if f32_layout == "rows":
            o_ref[:, cols] = val
        elif f32_layout == "transposed":
            o_ref[0, cols, :] = val.T
        elif f32_layout == "heads":
            o_ref[0, j] = val
        if ob_ref is not None:
            ob_ref[:, cols] = (val * mxu_copy_scale).astype(BF16)

    y = _dot(_rms(x_ref[...], g_ref[...], NORM_EPS).astype(BF16), w_ref[...])
    if not qk_epilogue:
        for j in range(y.shape[1] // LANES):
            emit(j, y[:, j * LANES:(j + 1) * LANES])
        return
    gh_ref, cos_ref, s1_ref, s2_ref = refs[3:7]
    gm = _group_matrix(1.0 / DIFF_DH)
    gh, cos, s1, s2 = gh_ref[...], cos_ref[...], s1_ref[...], s2_ref[...]
    half = ROT_DIM // 2
    for j in range(y.shape[1] // LANES):
        blk = y[:, j * LANES:(j + 1) * LANES]
        nb = blk * lax.rsqrt(_group_sum(blk * blk, gm) + NORM_EPS) * gh
        emit(j, nb * cos + pltpu.roll(nb, LANES - half, 1) * s1 + pltpu.roll(nb, half, 1) * s2)


def norm_mm(x, g, w, tm, rows_per_seq, rope=None, head_gain=None, mxu_copy_scale=None, f32_layout="rows"):
    m, d = x.shape
    n = w.shape[1]
    qk = rope is not None
    tiles_per_seq = rows_per_seq // tm
    seqs = m // rows_per_seq
    fixed = lambda i: (0, 0)
    in_specs = [pl.BlockSpec((tm, d), lambda i: (i, 0)), pl.BlockSpec((1, d), fixed), pl.BlockSpec((d, n), fixed)]
    args = [x, g.reshape(1, d), w]
    if qk:
        in_specs.append(pl.BlockSpec((1, LANES), fixed))
        args.append(jnp.tile(head_gain.reshape(1, DIFF_DH), (1, 2)))
        for t in rope:
            in_specs.append(pl.BlockSpec((tm, LANES), lambda i: (i % tiles_per_seq, 0)))
            args.append(t)
    out_specs, out_shape = [], []
    if f32_layout == "rows":
        out_specs.append(pl.BlockSpec((tm, n), lambda i: (i, 0)))
        out_shape.append(jax.ShapeDtypeStruct((m, n), F32))
    elif f32_layout == "transposed":
        out_specs.append(pl.BlockSpec((1, n, tm), lambda i: (i // tiles_per_seq, 0, i % tiles_per_seq)))
        out_shape.append(jax.ShapeDtypeStruct((seqs, n, rows_per_seq), F32))
    elif f32_layout == "heads":
        out_specs.append(pl.BlockSpec((1, n // LANES, tm, LANES),
                                      lambda i: (i // tiles_per_seq, 0, i % tiles_per_seq, 0)))
        out_shape.append(jax.ShapeDtypeStruct((seqs, n // LANES, rows_per_seq, LANES), F32))
    if mxu_copy_scale is not None:
        out_specs.append(pl.BlockSpec((tm, n), lambda i: (i, 0)))
        out_shape.append(jax.ShapeDtypeStruct((m, n), BF16))
    return pl.pallas_call(
        functools.partial(_norm_mm_kernel, qk_epilogue=qk, mxu_copy_scale=mxu_copy_scale, f32_layout=f32_layout),
        grid=(m // tm,),
        in_specs=in_specs,
        out_specs=out_specs,
        out_shape=out_shape,
        compiler_params=_params("parallel"),
        name="norm_mm_qk" if qk else "norm_mm",
    )(*args)


def rope_tables(pos):
    half = ROT_DIM // 2
    inv = ROPE_THETA ** (-jnp.arange(0, ROT_DIM, 2, dtype=F32) / ROT_DIM)
    ang = pos.astype(F32)[:, None] * inv[None, :]
    cos, sin = jnp.cos(ang), jnp.sin(ang)
    t = pos.shape[0]
    rest = DIFF_DH - ROT_DIM
    c64 = jnp.concatenate([cos, cos, jnp.ones((t, rest), F32)], axis=1)
    s1_64 = jnp.concatenate([-sin, jnp.zeros((t, DIFF_DH - half), F32)], axis=1)
    s2_64 = jnp.concatenate([jnp.zeros((t, half), F32), sin, jnp.zeros((t, rest), F32)], axis=1)
    return tuple(jnp.tile(z, (1, 2)) for z in (c64, s1_64, s2_64))


def _mem_kv_kernel(x_ref, g_ref, w_ref, gk_ref, k_ref, v_ref):
    h = _rms(x_ref[...], g_ref[...], NORM_EPS).astype(BF16)
    y = _dot(h, w_ref[...])
    gk = gk_ref[...]
    for j in range(MEM_HEADS):
        blk = y[:, j * MEM_DH:(j + 1) * MEM_DH]
        k_ref[:, j * MEM_DH:(j + 1) * MEM_DH] = _rms(blk, gk, NORM_EPS)
    v_ref[...] = y[:, MEM_W:]


def mem_kv(mem, g, w, gk):
    m, d = mem.shape
    return pl.pallas_call(
        _mem_kv_kernel,
        grid=(1,),
        in_specs=[
            pl.BlockSpec((m, d), lambda i: (0, 0)),
            pl.BlockSpec((1, d), lambda i: (0, 0)),
            pl.BlockSpec((d, 2 * MEM_W), lambda i: (0, 0)),
            pl.BlockSpec((1, MEM_DH), lambda i: (0, 0)),
        ],
        out_specs=[pl.BlockSpec((m, MEM_W), lambda i: (0, 0))] * 2,
        out_shape=[jax.ShapeDtypeStruct((m, MEM_W), F32)] * 2,
        compiler_params=_params("arbitrary"),
        name="mem_kv",
    )(mem, g.reshape(1, d), w, gk.reshape(1, MEM_DH))


def _mem_attn_kernel(q_ref, k_ref, v_ref, gq_ref, o_ref):
    q = q_ref[0]
    rows = q.shape[0]
    if rows < 8:
        q = jnp.broadcast_to(q, (8, q.shape[1]))
    k = k_ref[0]
    v = v_ref[0]
    gq = gq_ref[...]
    for h in range(MEM_HEADS):
        sl = slice(h * MEM_DH, (h + 1) * MEM_DH)
        qh = _rms(q[:, sl], gq, NORM_EPS).astype(BF16)
        s = _dot(qh, k[:, sl].astype(BF16), NT_DIMS) * (MEM_DH ** -0.5)
        p = jnp.exp(s - jnp.max(s, axis=-1, keepdims=True))
        o = _dot(p.astype(BF16), v[:, sl].astype(BF16)) / jnp.sum(p, axis=-1, keepdims=True)
        o_ref[0, :, sl] = o[:rows].astype(o_ref.dtype)


def mem_attn(proj, q_col_block, mk, mv, gq, tq):
    b, t, _ = proj.shape
    return pl.pallas_call(
        _mem_attn_kernel,
        grid=(b, t // tq),
        in_specs=[
            pl.BlockSpec((1, tq, MEM_W), lambda i, j: (i, j, q_col_block)),
            pl.BlockSpec((1, MEM_TOKENS, MEM_W), lambda i, j: (i, 0, 0)),
            pl.BlockSpec((1, MEM_TOKENS, MEM_W), lambda i, j: (i, 0, 0)),
            pl.BlockSpec((1, MEM_DH), lambda i, j: (0, 0)),
        ],
        out_specs=pl.BlockSpec((1, tq, MEM_W), lambda i, j: (i, j, 0)),
        out_shape=jax.ShapeDtypeStruct((b, t, MEM_W), BF16),
        compiler_params=_params("parallel", "arbitrary"),
        name="mem_attn",
    )(proj, mk, mv, gq.reshape(1, MEM_DH))


def _out_mm_kernel(x_ref, a_ref, b_ref, wa_ref, wb_ref, o_ref):
    o_ref[...] = x_ref[...] + _dot(a_ref[...], wa_ref[...]) + _dot(b_ref[...], wb_ref[...])


def out_mm(x, a, b, w, layer, tm):
    m, d = x.shape
    tn = d
    ka, kb = a.shape[1], b.shape[1]
    kb_blocks = ka // kb
    return pl.pallas_call(
        _out_mm_kernel,
        grid=(m // tm, d // tn),
        in_specs=[
            pl.BlockSpec((tm, tn), lambda i, j: (i, j)),
            pl.BlockSpec((tm, ka), lambda i, j: (i, 0)),
            pl.BlockSpec((tm, kb), lambda i, j: (i, 0)),
            pl.BlockSpec((None, ka, tn), lambda i, j: (layer, 0, j)),
            pl.BlockSpec((None, kb, tn), lambda i, j: (layer, kb_blocks, j)),
        ],
        out_specs=pl.BlockSpec((tm, tn), lambda i, j: (i, j)),
        out_shape=jax.ShapeDtypeStruct((m, d), F32),
        compiler_params=_params("parallel", "arbitrary"),
        name="out_mm",
    )(x, a, b, w, w)


def _a_prep_kernel(x_ref, xs_ref, sp_ref, g_ref, mu_ref, xn_ref, cat_ref, mix_ref, *, tiles_per_seq):
    g = g_ref[...]
    xn = _rms(x_ref[...], g, NORM_EPS)
    if tiles_per_seq is None:
        xp = sp_ref[...]
    else:
        xp = _rms(xs_ref[...], g, NORM_EPS)
        first = pl.program_id(0) % tiles_per_seq == 0
        row = lax.broadcasted_iota(jnp.int32, xn.shape, 0)
        xp = jnp.where(jnp.logical_and(row == 0, first), sp_ref[0], xp)
    xx = xp - xn
    d = xn.shape[1]
    xn_ref[...] = xn
    cat_ref[:, :d] = xn.astype(BF16)
    cat_ref[:, d:] = xx.astype(BF16)
    for i in range(3):
        mix_ref[:, i * d:(i + 1) * d] = (xn + xx * mu_ref[i:i + 1, :]).astype(BF16)


def a_prep(x, x_shift, shift_prev, g, mu, tm, rows_per_seq):
    m, d = x.shape
    if rows_per_seq == 1:
        tiles_per_seq = None
        x_shift = x
        sp, sp_spec = shift_prev, pl.BlockSpec((tm, d), lambda i: (i, 0))
    else:
        tiles_per_seq = rows_per_seq // tm
        sp = shift_prev.reshape(shift_prev.shape[0], 1, d)
        sp_spec = pl.BlockSpec((1, 1, d), lambda i: (i // tiles_per_seq, 0, 0))
    return pl.pallas_call(
        functools.partial(_a_prep_kernel, tiles_per_seq=tiles_per_seq),
        grid=(m // tm,),
        in_specs=[
            pl.BlockSpec((tm, d), lambda i: (i, 0)),
            pl.BlockSpec((tm, d), lambda i: (i, 0)),
            sp_spec,
            pl.BlockSpec((1, d), lambda i: (0, 0)),
            pl.BlockSpec((3, d), lambda i: (0, 0)),
        ],
        out_specs=[
            pl.BlockSpec((tm, d), lambda i: (i, 0)),
            pl.BlockSpec((tm, 2 * d), lambda i: (i, 0)),
            pl.BlockSpec((tm, 3 * d), lambda i: (i, 0)),
        ],
        out_shape=[
            jax.ShapeDtypeStruct((m, d), F32),
            jax.ShapeDtypeStruct((m, 2 * d), BF16),
            jax.ShapeDtypeStruct((m, 3 * d), BF16),
        ],
        compiler_params=_params("parallel"),
        name="a_prep",
    )(x, x_shift, sp, g.reshape(1, d), mu)


def _mm_kernel(a_ref, w_ref, o_ref):
    o_ref[...] = _dot(a_ref[...], w_ref[...])


def mm(a, w, layer, tm, tn=1024):
    m, k = a.shape
    n = w.shape[2]
    return pl.pallas_call(
        _mm_kernel,
        grid=(m // tm, n // tn),
        in_specs=[pl.BlockSpec((tm, k), lambda i, j: (i, 0)),
                  pl.BlockSpec((None, k, tn), lambda i, j: (layer, 0, j))],
        out_specs=pl.BlockSpec((tm, tn), lambda i, j: (i, j)),
        out_shape=jax.ShapeDtypeStruct((m, n), F32),
        compiler_params=_params("parallel", "arbitrary"),
        name="mm",
    )(a, w)


def _a_mix_kernel(k_ref, mix_ref, w1_ref, w2_ref, a1_ref, a2_ref, g1_ref, g2_ref, w0_ref, a0_ref, kk_ref,
                  ka_ref, ld_ref, kp_ref, kn_ref, b_ref, g_ref):
    d = D_MODEL
    xw, xa, xg = mix_ref[:, :d], mix_ref[:, d:2 * d], mix_ref[:, 2 * d:]
    wl = _dot(jnp.tanh(_dot(xw, w1_ref[...])).astype(BF16), w2_ref[...]) + w0_ref[...]
    ld_ref[...] = -math.exp(-0.5) * _sigmoid(wl)
    a = _sigmoid(_dot(_dot(xa, a1_ref[...]).astype(BF16), a2_ref[...]) + a0_ref[...])
    g_ref[...] = _dot(_sigmoid(_dot(xg, g1_ref[...])).astype(BF16), g2_ref[...])
    k = k_ref[...]
    kp_ref[...] = k * (1.0 + (a - 1.0) * ka_ref[...])
    kraw = k * kk_ref[...]
    gm = _group_matrix(1.0)
    for j in range(RWKV_W // LANES):
        sl = slice(j * LANES, (j + 1) * LANES)
        blk = kraw[:, sl]
        kn = blk / jnp.maximum(jnp.sqrt(_group_sum(blk * blk, gm)), 1e-12)
        kn_ref[:, sl] = kn
        b_ref[:, sl] = kn * a[:, sl]


def a_mix(proj, mix, lw, tm):
    m = proj.shape[0]
    row = lambda i: (i, 0)
    fixed = lambda i: (0, 0)
    vec = pl.BlockSpec((1, RWKV_W), fixed)
    out = jax.ShapeDtypeStruct((m, RWKV_W), F32)
    return pl.pallas_call(
        _a_mix_kernel,
        grid=(m // tm,),
        in_specs=[
            pl.BlockSpec((tm, RWKV_W), lambda i: (i, 1)),
            pl.BlockSpec((tm, 3 * D_MODEL), row),
            pl.BlockSpec((D_MODEL, LORA_PAD), fixed), pl.BlockSpec((LORA_PAD, RWKV_W), fixed),
            pl.BlockSpec((D_MODEL, LORA_PAD), fixed), pl.BlockSpec((LORA_PAD, RWKV_W), fixed),
            pl.BlockSpec((D_MODEL, LORA_G), fixed), pl.BlockSpec((LORA_G, RWKV_W), fixed),
            vec, vec, vec, vec,
        ],
        out_specs=[pl.BlockSpec((tm, RWKV_W), row)] * 5,
        out_shape=[out] * 5,
        compiler_params=_params("parallel"),
        name="a_mix",
    )(proj, mix, lw["w1"], lw["w2"], lw["a1"], lw["a2"], lw["g1"], lw["g2"],
      lw["w0"], lw["a0"], lw["k_k"], lw["k_a"])


def _wkv_masks(c):
    c2 = 2 * c
    ri = lax.broadcasted_iota(jnp.int32, (c2, c2), 0)
    ci = lax.broadcasted_iota(jnp.int32, (c2, c2), 1)
    same = (ri // c) == (ci // c)
    masks = dict(
        in_head0=lax.broadcasted_iota(jnp.int32, (c, LANES), 1) < RWKV_N,
        tri=jnp.where(lax.broadcasted_iota(jnp.int32, (c, c), 0) >= lax.broadcasted_iota(jnp.int32, (c, c), 1),
                      1.0, 0.0).astype(BF16),
        strict=jnp.logical_and(same, (ri % c) > (ci % c)),
        incl=jnp.logical_and(same, (ri % c) >= (ci % c)),
        eye=jnp.where(ri == ci, 1.0, 0.0),
        diag8=(ri // 8) == (ci // 8),
        lower_left=[],
    )
    size = 8
    while size < c:
        masks["lower_left"].append(jnp.logical_and(
            (ri // (2 * size)) == (ci // (2 * size)),
            jnp.logical_and((ri // size) % 2 == 1, (ci // size) % 2 == 0)))
        size *= 2
    return masks


def _each(f, *lists):
    return [f(*args) for args in zip(*lists)]


def _wkv_chunk(s, r, ld, k, v, kn, b, mk):
    c = r[0].shape[0]
    c2 = 2 * c
    in_head0 = mk["in_head0"]
    tri = mk["tri"]

    def stack(z):
        return jnp.concatenate([jnp.where(in_head0, z, 0.0), jnp.where(in_head0, 0.0, z)], axis=0)

    def bdot(x, y):
        return _dot(x.astype(BF16), y.astype(BF16))

    p1 = _each(lambda z: z.astype(BF16), ld)
    rem = _each(lambda z, p: z - p.astype(F32), ld, p1)
    p2 = _each(lambda z: z.astype(BF16), rem)
    p3 = _each(lambda z, p: (z - p.astype(F32)).astype(BF16), rem, p2)
    cum = _each(lambda a1, a2, a3: _dot(tri, a1) + _dot(tri, a2) + _dot(tri, a3), p1, p2, p3)
    eg = _each(jnp.exp, cum)
    einv = _each(lambda z: jnp.exp(-z), cum)
    at_s = _each(lambda n, z, d: stack(-n * jnp.exp(z - d)).astype(BF16), kn, cum, ld)
    rt_s = _each(lambda x, e: stack(x * e).astype(BF16), r, eg)
    bt = _each(lambda x, e: x * e, b, einv)
    kt = _each(lambda x, e: x * e, k, einv)
    v_s = _each(lambda x: stack(x).astype(BF16), v)

    gmat = _each(lambda a, x, y, z: _dot(jnp.concatenate([a, x], axis=0),
                                         jnp.concatenate([y, y, z, z], axis=0).astype(BF16), NT_DIMS),
                 at_s, rt_s, bt, kt)
    a_ab = _each(lambda g: jnp.where(mk["strict"], g[:c2, :c2], 0.0), gmat)
    a_ak = _each(lambda g: jnp.where(mk["strict"], g[:c2, c2:], 0.0).astype(BF16), gmat)
    a_rr = _each(lambda g: jnp.concatenate([jnp.where(mk["incl"], g[c2:, :c2], 0.0),
                                            jnp.where(mk["incl"], g[c2:, c2:], 0.0)], axis=1).astype(BF16), gmat)

    ad = _each(lambda a: jnp.where(mk["diag8"], a, 0.0), a_ab)
    ad2 = _each(bdot, ad, ad)
    ad4 = _each(bdot, ad2, ad2)
    inv = _each(lambda a: mk["eye"] + a, ad)
    inv = _each(lambda x, y: x + bdot(x, y), inv, ad2)
    inv = _each(lambda x, y: x + bdot(x, y), inv, ad4)
    for lower_left in mk["lower_left"]:
        half = _each(lambda x, a: bdot(x, jnp.where(lower_left, a, 0.0)), inv, a_ab)
        inv = _each(lambda x, y: x + bdot(y, x), inv, half)

    s_b = _each(lambda z: z.astype(BF16), s)
    rhs = _each(lambda a, x, y, z: _dot(a, x) + _dot(y, z, NT_DIMS), a_ak, v_s, at_s, s_b)
    u = _each(lambda x, y: bdot(x, y).astype(BF16), inv, rhs)
    uv = _each(lambda x, y: jnp.concatenate([x, y], axis=0), u, v_s)
    y_s = _each(lambda x, z, a, w: _dot(x, z, NT_DIMS) + _dot(a, w), rt_s, s_b, a_rr, uv)
    ds = _each(lambda w, x, y: _dot(w, jnp.concatenate([stack(x), stack(y)], axis=0).astype(BF16), TN_DIMS),
               uv, bt, kt)
    y = _each(lambda z: z[:c] + z[c:], y_s)
    s_new = _each(lambda z, dz, e: (z + dz) * e[c - 1:c, :], s, ds, eg)
    return y, s_new


def _wkv_kernel(r_ref, ld_ref, k_ref, v_ref, kn_ref, b_ref, y_ref, s_ref, st_ref, *, chunks, pairs):
    @pl.when(pl.program_id(2) == 0)
    def _():
        st_ref[...] = jnp.zeros_like(st_ref)

    mk = _wkv_masks(CHUNK)
    s = [st_ref[p] for p in range(pairs)]
    for i in range(chunks):
        rows = slice(i * CHUNK, (i + 1) * CHUNK)
        cut = lambda ref: [ref[0, rows, p * LANES:(p + 1) * LANES] for p in range(pairs)]
        y, s = _wkv_chunk(s, cut(r_ref), cut(ld_ref), cut(k_ref), cut(v_ref), cut(kn_ref), cut(b_ref), mk)
        for p in range(pairs):
            y_ref[0, rows, p * LANES:(p + 1) * LANES] = y[p]
    for p in range(pairs):
        st_ref[p] = s[p]

    @pl.when(pl.program_id(2) == pl.num_programs(2) - 1)
    def _():
        s_ref[0] = st_ref[...]


WKV_PAIRS = 12
WKV_CHUNKS = 2


def wkv_scan(proj, ld, kp, kn, b):
    bsz, t, _ = proj.shape
    groups = RWKV_W // LANES // WKV_PAIRS
    tb = WKV_CHUNKS * CHUNK
    width = WKV_PAIRS * LANES
    blk = lambda off: pl.BlockSpec((1, tb, width), lambda i, p, c: (i, c, p + off))
    return pl.pallas_call(
        functools.partial(_wkv_kernel, chunks=WKV_CHUNKS, pairs=WKV_PAIRS),
        grid=(bsz, groups, t // tb),
        in_specs=[blk(0), blk(0), blk(0), blk(2 * groups), blk(0), blk(0)],
        out_specs=[
            pl.BlockSpec((1, tb, width), lambda i, p, c: (i, c, p)),
            pl.BlockSpec((1, WKV_PAIRS, LANES, LANES), lambda i, p, c: (i, p, 0, 0)),
        ],
        out_shape=[
            jax.ShapeDtypeStruct((bsz, t, RWKV_W), F32),
            jax.ShapeDtypeStruct((bsz, RWKV_W // LANES, LANES, LANES), F32),
        ],
        scratch_shapes=[pltpu.VMEM((WKV_PAIRS, LANES, LANES), F32)],
        compiler_params=_params("parallel", "parallel", "arbitrary"),
        name="wkv_scan",
    )(proj, ld, kp, proj, kn, b)


def _wkv_step_kernel(s_ref, r_ref, ld_ref, k_ref, v_ref, kn_ref, b_ref, y_ref, so_ref):
    s = s_ref[0]
    sa = jnp.sum(s * (-kn_ref[0]), axis=-1, keepdims=True)
    s = s * jnp.exp(ld_ref[0]) + sa * b_ref[0] + v_ref[0] * k_ref[0]
    so_ref[0] = s
    y_ref[0] = jnp.sum(s * r_ref[0], axis=-1, keepdims=True)


def wkv_step(s0, r, ld, k, v, kn, b):
    bsz = s0.shape[0]
    h, n = RWKV_HEADS, RWKV_N
    st = pl.BlockSpec((1, h, n, n), lambda i: (i, 0, 0, 0))
    rw = pl.BlockSpec((1, h, 1, n), lambda i: (i, 0, 0, 0))
    cl = pl.BlockSpec((1, h, n, 1), lambda i: (i, 0, 0, 0))
    return pl.pallas_call(
        _wkv_step_kernel,
        grid=(bsz,),
        in_specs=[st, rw, rw, rw, cl, rw, rw],
        out_specs=[cl, st],
        out_shape=[jax.ShapeDtypeStruct((bsz, h, n, 1), F32), jax.ShapeDtypeStruct((bsz, h, n, n), F32)],
        compiler_params=_params("parallel"),
        name="wkv_step",
    )(s0, r, ld, k, v, kn, b)


def _a_post_kernel(y_ref, r_ref, kp_ref, v_ref, g_ref, lw_ref, lb_ref, rk_ref, o_ref):
    gsum = _group_matrix(1.0)
    gmean = _group_matrix(1.0 / RWKV_N)
    for j in range(RWKV_W // LANES):
        sl = slice(j * LANES, (j + 1) * LANES)
        y = y_ref[:, sl]
        cen = y - _group_sum(y, gmean)
        yn = cen * lax.rsqrt(_group_sum(cen * cen, gmean) + LNX_EPS) * lw_ref[:, sl] + lb_ref[:, sl]
        bonus = _group_sum(r_ref[:, sl] * kp_ref[:, sl] * rk_ref[:, sl], gsum) * v_ref[:, sl]
        o_ref[:, sl] = ((yn + bonus) * g_ref[:, sl]).astype(o_ref.dtype)


def a_post(y, proj, kp, g, lnx_w, lnx_b, r_k, tm):
    m = y.shape[0]
    row = lambda i: (i, 0)
    vec = pl.BlockSpec((1, RWKV_W), lambda i: (0, 0))
    tile = pl.BlockSpec((tm, RWKV_W), row)
    return pl.pallas_call(
        _a_post_kernel,
        grid=(m // tm,),
        in_specs=[tile, pl.BlockSpec((tm, RWKV_W), lambda i: (i, 0)), tile,
                  pl.BlockSpec((tm, RWKV_W), lambda i: (i, 2)), tile, vec, vec, vec],
        out_specs=tile,
        out_shape=jax.ShapeDtypeStruct((m, RWKV_W), BF16),
        compiler_params=_params("parallel"),
        name="a_post",
    )(y, proj, kp, proj, g, lnx_w.reshape(1, RWKV_W), lnx_b.reshape(1, RWKV_W), r_k.reshape(1, RWKV_W))


def _lambda(lam_ref, lam_init):
    lq = lam_ref[...]
    l1 = jnp.sum(lq[0:1] * lq[1:2], axis=-1, keepdims=True)
    l2 = jnp.sum(lq[2:3] * lq[3:4], axis=-1, keepdims=True)
    return jnp.exp(l1) - jnp.exp(l2) + lam_init


ATTN_HEADS = 2
ATTN_TQ = 512
LOG2E = 1.4426950408889634
Q_SCALE = DIFF_DH ** -0.5 * LOG2E


def _diff_attn_kernel(qi_ref, ki_ref, q_ref, k_ref, v_ref, lam_ref, gs_ref, o_ref, m_ref, l_ref, acc_ref, *,
                      tq, lam_init):
    qi = qi_ref[pl.program_id(2)]
    ki = ki_ref[pl.program_id(2)]

    @pl.when(ki == 0)
    def _():
        m_ref[...] = jnp.full_like(m_ref, -jnp.inf)
        l_ref[...] = jnp.zeros_like(l_ref)
        acc_ref[...] = jnp.zeros_like(acc_ref)

    def step(diagonal):
        lane = lax.broadcasted_iota(jnp.int32, (tq, DIFF_DV), 1)
        if diagonal:
            visible = (lax.broadcasted_iota(jnp.int32, (tq, tq), 1)
                       <= lax.broadcasted_iota(jnp.int32, (tq, tq), 0))
        for h in range(ATTN_HEADS):
            cols = slice(h * DIFF_DV, (h + 1) * DIFF_DV)
            q = q_ref[0, :, cols]
            kb = k_ref[0, :, cols]
            v1 = jnp.concatenate([v_ref[0, :, cols], jnp.ones((tq, LANES), BF16)], axis=1)
            for c in range(2):
                idx = 2 * h + c
                in_comp = (lane < DIFF_DH) if c == 0 else (lane >= DIFF_DH)
                s = _dot(jnp.where(in_comp, q, jnp.zeros_like(q)), kb, NT_DIMS)
                if diagonal:
                    s = jnp.where(visible, s, -jnp.inf)
                m_old = m_ref[idx]
                m_new = jnp.maximum(m_old, jnp.max(s, axis=-1, keepdims=True))
                alpha = jnp.exp2(m_old - m_new)
                p = jnp.exp2(s - jnp.concatenate([m_new] * (tq // LANES), axis=1))
                pv = _dot(p.astype(BF16), v1)
                l_ref[idx] = alpha * l_ref[idx] + pv[:, DIFF_DV:]
                acc_ref[idx] = alpha * acc_ref[idx] + pv[:, :DIFF_DV]
                m_ref[idx] = m_new

    @pl.when(ki < qi)
    def _():
        step(False)

    @pl.when(ki == qi)
    def _():
        step(True)
        lam = _lambda(lam_ref, lam_init)
        for h in range(ATTN_HEADS):
            o = acc_ref[2 * h] / l_ref[2 * h] - lam * (acc_ref[2 * h + 1] / l_ref[2 * h + 1])
            o_ref[0, :, h * DIFF_DV:(h + 1) * DIFF_DV] = (
                _rms(o, gs_ref[...], SUBLN_EPS) * (1.0 - lam_init)).astype(o_ref.dtype)


def diff_attn_prompt(q, k, v, lam_p, subln, lam_init):
    b, t, _ = q.shape
    tq = ATTN_TQ
    width = ATTN_HEADS * DIFF_DV
    chains = 2 * ATTN_HEADS
    blocks = t // tq
    pairs = [(qi, ki) for qi in range(blocks) for ki in range(qi + 1)]
    q_of = jnp.asarray([p[0] for p in pairs], jnp.int32)
    k_of = jnp.asarray([p[1] for p in pairs], jnp.int32)
    q_map = lambda i, h, p, q_of, k_of: (i, q_of[p], h)
    kv_map = lambda i, h, p, q_of, k_of: (i, k_of[p], h)
    fixed = lambda i, h, p, q_of, k_of: (0, 0)
    grid_spec = pltpu.PrefetchScalarGridSpec(
        num_scalar_prefetch=2,
        grid=(b, DIFF_HEADS // ATTN_HEADS, len(pairs)),
        in_specs=[
            pl.BlockSpec((1, tq, width), q_map),
            pl.BlockSpec((1, tq, width), kv_map),
            pl.BlockSpec((1, tq, width), kv_map),
            pl.BlockSpec((4, DIFF_DH), fixed),
            pl.BlockSpec((1, DIFF_DV), fixed),
        ],
        out_specs=pl.BlockSpec((1, tq, width), q_map),
        scratch_shapes=[pltpu.VMEM((chains, tq, LANES), F32), pltpu.VMEM((chains, tq, LANES), F32),
                        pltpu.VMEM((chains, tq, DIFF_DV), F32)],
    )
    return pl.pallas_call(
        functools.partial(_diff_attn_kernel, tq=tq, lam_init=lam_init),
        grid_spec=grid_spec,
        out_shape=jax.ShapeDtypeStruct((b, t, DIFF_W), BF16),
        compiler_params=_params("parallel", "parallel", "arbitrary"),
        name="diff_attn_prompt",
    )(q_of, k_of, q, k, v, lam_p, subln.reshape(1, DIFF_DV))


DEC_PAGES = 4
DEC_ROWS = 8


def _dec_attn_kernel(pt_ref, q_ref, kn_ref, vn_ref, lam_ref, gs_ref, *refs, lam_init):
    k_refs = refs[:DEC_PAGES]
    v_refs = refs[DEC_PAGES:2 * DEC_PAGES]
    o_ref, qb_ref, m_ref, l_ref, acc_ref = refs[2 * DEC_PAGES:]
    step = pl.program_id(1)
    row = lax.broadcasted_iota(jnp.int32, (DIFF_HEADS, DEC_ROWS, LANES), 1)

    def comp_rows(prod):
        s0 = jnp.sum(prod[:, :DIFF_DH, :], axis=1, keepdims=True)
        s1 = jnp.sum(prod[:, DIFF_DH:, :], axis=1, keepdims=True)
        return jnp.where(row == 0, s0, jnp.where(row == 1, s1, 0.0))

    @pl.when(step == 0)
    def _():
        q_col = q_ref[0] * (DIFF_DH ** -0.5)
        qb_ref[...] = jnp.broadcast_to(q_col, qb_ref.shape)
        m_ref[...] = comp_rows(q_col * kn_ref[0])
        l_ref[...] = jnp.ones_like(l_ref)
        acc_ref[...] = jnp.broadcast_to(vn_ref[0], acc_ref.shape)

    qb = qb_ref[...]
    for k_ref, v_ref in zip(k_refs, v_refs):
        s = comp_rows(k_ref[0] * qb)
        m_old = m_ref[...]
        m_new = jnp.maximum(m_old, jnp.max(s, axis=-1, keepdims=True))
        alpha = jnp.exp(m_old - m_new)
        p = jnp.exp(s - m_new)
        l_ref[...] = alpha * l_ref[...] + jnp.sum(p, axis=-1, keepdims=True)
        pv = lax.dot_general(p.astype(BF16), v_ref[0].astype(BF16), (((2,), (1,)), ((0,), (0,))),
                             preferred_element_type=F32)
        acc_ref[...] = alpha * acc_ref[...] + pv
        m_ref[...] = m_new

    @pl.when(step == pl.num_programs(1) - 1)
    def _():
        lam = _lambda(lam_ref, lam_init)
        w = acc_ref[...] / l_ref[...]
        o = w[:, 0:1, :] - lam * w[:, 1:2, :]
        o = o * lax.rsqrt(jnp.mean(o * o, axis=-1, keepdims=True) + SUBLN_EPS) * gs_ref[...] * (1.0 - lam_init)
        o_ref[0] = jnp.broadcast_to(o, o_ref.shape[1:])


def diff_attn_decode(q_col, k_col, v_new, cache_kt, cache_v, page_table, lam_p, subln, lam_init):
    b = q_col.shape[0]
    n_pages = page_table.shape[1]
    col = pl.BlockSpec((1, DIFF_HEADS, DIFF_DV, 1), lambda i, s, pt: (i, 0, 0, 0))
    state = pltpu.VMEM((DIFF_HEADS, DEC_ROWS, LANES), F32)

    def page_spec(j):
        return pl.BlockSpec((1, DIFF_HEADS, PAGE_SIZE, DIFF_DV), lambda i, s, pt: (pt[i, s * DEC_PAGES + j], 0, 0, 0))

    grid_spec = pltpu.PrefetchScalarGridSpec(
        num_scalar_prefetch=1,
        grid=(b, n_pages // DEC_PAGES),
        in_specs=[col, col,
                  pl.BlockSpec((1, DIFF_HEADS, 1, DIFF_DV), lambda i, s, pt: (i, 0, 0, 0)),
                  pl.BlockSpec((4, DIFF_DH), lambda i, s, pt: (0, 0)),
                  pl.BlockSpec((1, DIFF_DV), lambda i, s, pt: (0, 0))]
        + [page_spec(j) for j in range(DEC_PAGES)] * 2,
        out_specs=pl.BlockSpec((1, DIFF_HEADS, DEC_ROWS, DIFF_DV), lambda i, s, pt: (i, 0, 0, 0)),
        scratch_shapes=[pltpu.VMEM((DIFF_HEADS, DIFF_DV, PAGE_SIZE), F32), state, state, state],
    )
    return pl.pallas_call(
        functools.partial(_dec_attn_kernel, lam_init=lam_init),
        grid_spec=grid_spec,
        out_shape=jax.ShapeDtypeStruct((b, DIFF_HEADS, DEC_ROWS, DIFF_DV), F32),
        compiler_params=_params("parallel", "arbitrary"),
        name="diff_attn_decode",
    )(page_table, q_col, k_col, v_new, lam_p, subln.reshape(1, DIFF_DV),
      *([cache_kt] * DEC_PAGES), *([cache_v] * DEC_PAGES))


def _pad_lora(w_in, w_out):
    pad = LORA_PAD - w_in.shape[1]
    return (jnp.pad(w_in, ((0, 0), (0, pad))).astype(BF16), jnp.pad(w_out, ((0, pad), (0, 0))).astype(BF16))


def kernel(x_prompt, x_sample, mem_prompt, state_wkv, state_shift, cache_mem_k, cache_mem_v, cache_k, cache_v, page_table, ffn_norm, ffn_w13, ffn_w2, mix_norm, w_out, mem_norm, mem_w_kv, mem_q_norm, mem_k_norm, a_w_in, a_mu, a_w0, a_w1, a_w2, a_a0, a_a1, a_a2, a_g1, a_g2, a_k_k, a_k_a, a_r_k, a_lnx_w, a_lnx_b, kv_norm, kv_w, k_norm, b_w_in, b_q_norm, b_lam, b_subln):
    d = D_MODEL
    wout_b = w_out.astype(BF16)
    memw_b = mem_w_kv.astype(BF16)
    awin_b = a_w_in.astype(BF16)
    kvw_k, kvw_v = kv_w[:, :DIFF_W].astype(BF16), kv_w[:, DIFF_W:].astype(BF16)
    bq_b, bm_b = b_w_in[:, :, :DIFF_W].astype(BF16), b_w_in[:, :, DIFF_W:].astype(BF16)
    loras = []
    for i in range(N_A):
        w1, w2 = _pad_lora(a_w1[i], a_w2[i])
        a1, a2 = _pad_lora(a_a1[i], a_a2[i])
        loras.append(dict(w1=w1, w2=w2, a1=a1, a2=a2, g1=a_g1[i].astype(BF16), g2=a_g2[i].astype(BF16),
                          w0=a_w0[i].reshape(1, RWKV_W), a0=a_a0[i].reshape(1, RWKV_W),
                          k_k=a_k_k[i].reshape(1, RWKV_W), k_a=a_k_a[i].reshape(1, RWKV_W)))

    def run(x3, shift_prev, wkv0, mk, mv, pos, decode):
        bsz, t, _ = x3.shape
        m = bsz * t
        tm = min(512, m)
        x = x3.reshape(m, d)
        rope = rope_tables(jnp.broadcast_to(pos, (m,)) if decode else pos)
        rope_rows = m if decode else t
        shifts, states = [], []
        for i in range(N_A):
            x = yield x, i, 0
            x_shift = jnp.concatenate([jnp.zeros((bsz, 1, d), F32), x.reshape(bsz, t, d)[:, :-1]], axis=1)
            xn, cat, mix = a_prep(x, x_shift.reshape(m, d), shift_prev[i], mix_norm[i], a_mu[i], tm, t)
            shifts.append(xn.reshape(bsz, t, d)[:, -1])
            proj = mm(cat, awin_b, i, tm)
            ld, kp, kn, bvec, gate = a_mix(proj, mix, loras[i], min(256, m))
            proj3 = proj.reshape(bsz, t, A_IN)
            if decode:
                heads = lambda z: z.reshape(bsz, RWKV_HEADS, 1, RWKV_N)
                y, s_new = wkv_step(
                    wkv0[i], heads(proj[:, :RWKV_W]), heads(ld), heads(kp),
                    proj[:, 2 * RWKV_W:3 * RWKV_W].reshape(bsz, RWKV_HEADS, RWKV_N, 1), heads(kn), heads(bvec))
                y = y.reshape(m, RWKV_W)
            else:
                r3 = lambda z: z.reshape(bsz, t, RWKV_W)
                y, s_pairs = wkv_scan(proj3, r3(ld), r3(kp), r3(kn), r3(bvec))
                y = y.reshape(m, RWKV_W)
                s_new = jnp.stack([s_pairs[:, :, :RWKV_N, :RWKV_N], s_pairs[:, :, RWKV_N:, RWKV_N:]], axis=2)
                s_new = s_new.reshape(bsz, RWKV_HEADS, RWKV_N, RWKV_N)
            states.append(s_new)
            y_mix = a_post(y, proj, kp, gate, a_lnx_w[i], a_lnx_b[i], a_r_k[i], min(256, m))
            o_mem = mem_attn(proj3, 3 * RWKV_W // MEM_W, mk[i], mv[i], mem_q_norm[i], min(512, t))
            x = out_mm(x, y_mix, o_mem.reshape(m, MEM_W), wout_b, i, tm)
            x = yield x, i, 1
        if decode:
            k_rows, = norm_mm(x, kv_norm, kvw_k, tm, rope_rows, rope, k_norm)
            v_rows, = norm_mm(x, kv_norm, kvw_v, tm, rope_rows)
            k_col = k_rows.reshape(bsz, DIFF_HEADS, DIFF_DV, 1)
            v_row = v_rows.reshape(bsz, DIFF_HEADS, 1, DIFF_DV)
            k_out = k_rows.reshape(bsz, t, DIFF_HEADS, 2, DIFF_DH)
            v_out = v_rows.reshape(bsz, t, DIFF_HEADS, DIFF_DV)
        else:
            k_t, k_b = norm_mm(x, kv_norm, kvw_k, tm, rope_rows, rope, k_norm, mxu_copy_scale=1.0,
                               f32_layout="transposed")
            v_h, v_b = norm_mm(x, kv_norm, kvw_v, tm, rope_rows, mxu_copy_scale=1.0, f32_layout="heads")
            k_b, v_b = k_b.reshape(bsz, t, DIFF_W), v_b.reshape(bsz, t, DIFF_W)
            k_out = jnp.transpose(k_t.reshape(bsz, DIFF_HEADS, 2, DIFF_DH, t), (0, 4, 1, 2, 3))
            v_out = jnp.transpose(v_h, (0, 2, 1, 3))
        for j in range(N_B):
            i = N_A + j
            lam_init = 0.8 - 0.6 * math.exp(-0.3 * i)
            x = yield x, i, 0
            q_mem = norm_mm(x, mix_norm[i], bm_b[j], tm, rope_rows)[0].reshape(bsz, t, MEM_W)
            if decode:
                q, = norm_mm(x, mix_norm[i], bq_b[j], tm, rope_rows, rope, b_q_norm[j])
                o = diff_attn_decode(q.reshape(bsz, DIFF_HEADS, DIFF_DV, 1), k_col, v_row, cache_kt, cache_vt,
                                     page_table, b_lam[j], b_subln[j], lam_init)
                o = o[:, :, 0, :].astype(BF16)
            else:
                q_b, = norm_mm(x, mix_norm[i], bq_b[j], tm, rope_rows, rope, b_q_norm[j], mxu_copy_scale=Q_SCALE,
                               f32_layout=None)
                o = diff_attn_prompt(q_b.reshape(bsz, t, DIFF_W), k_b, v_b, b_lam[j], b_subln[j], lam_init)
            o_mem = mem_attn(q_mem, 0, mk[i], mv[i], mem_q_norm[i], min(512, t))
            x = out_mm(x, o.reshape(m, DIFF_W), o_mem.reshape(m, MEM_W), wout_b, i, tm)
            x = yield x, i, 1
        return x.reshape(bsz, t, d), jnp.stack(states), jnp.stack(shifts), k_out, v_out

    n_pool = cache_k.shape[0]
    cache_kt = jnp.transpose(cache_k, (0, 2, 3, 4, 1)).reshape(n_pool, DIFF_HEADS, DIFF_DV, PAGE_SIZE)
    cache_vt = jnp.transpose(cache_v, (0, 2, 1, 3))

    bp, tp, _ = x_prompt.shape
    bs, ts, _ = x_sample.shape
    assert ts == 1, "the sample group is decoded one token per sequence"
    mem2 = mem_prompt.reshape(bp * MEM_TOKENS, d)
    mk_p, mv_p = [], []
    for i in range(DEPTH):
        mk_i, mv_i = mem_kv(mem2, mem_norm[i], memw_b[i], mem_k_norm[i])
        mk_p.append(mk_i.reshape(bp, MEM_TOKENS, MEM_W))
        mv_p.append(mv_i.reshape(bp, MEM_TOKENS, MEM_W))

    mk_s = cache_mem_k.reshape(DEPTH, bs, MEM_TOKENS, MEM_W)
    mv_s = cache_mem_v.reshape(DEPTH, bs, MEM_TOKENS, MEM_W)
    groups = [run(x_prompt, jnp.zeros((N_A, bp, d), F32), None, mk_p, mv_p, jnp.arange(tp), decode=False),
              run(x_sample, state_shift, state_wkv, mk_s, mv_s, jnp.full((1,), PAST_LEN, jnp.int32), decode=True)]
    requests = [next(gen) for gen in groups]
    results = [None, None]
    while results[0] is None:
        (x_p, layer, half), (x_s, layer_s, half_s) = requests
        assert (layer, half) == (layer_s, half_s)
        outs = ffn(x_p, x_s, ffn_norm[layer, half], ffn_w13, ffn_w2, layer, half)
        for n, gen in enumerate(groups):
            try:
                requests[n] = gen.send(outs[n])
            except StopIteration as done:
                results[n] = done.value
    assert results[1] is not None
    (y_p, wkv_p, shift_p, k_p, v_p), (y_s, wkv_s, shift_s, k_s, v_s) = results

    memshape = lambda zs: jnp.stack(zs).reshape(DEPTH, bp, MEM_TOKENS, MEM_HEADS, MEM_DH)
    return (y_p, y_s, wkv_p, shift_p, wkv_s, shift_s, k_p, v_p, k_s, v_s, memshape(mk_p), memshape(mv_p))
```

```python
import functools
import math

import jax
import jax.numpy as jnp
from jax import lax
from jax.experimental import pallas as pl
from jax.experimental.pallas import tpu as pltpu

F32 = jnp.float32
BF16 = jnp.bfloat16

D_MODEL = 2048
DEPTH = 4
N_A = 2
N_B = 2
MEM_TOKENS = 256
MEM_HEADS = 4
MEM_W = 512
MEM_DH = 128
RWKV_W = 1536
RWKV_N = 64
RWKV_HEADS = 24
LORA_PAD = 128
LORA_G = 256
A_IN = 3 * RWKV_W + MEM_W
DIFF_W = 1536
DIFF_DV = 128
DIFF_HEADS = 12
DIFF_DH = 64
ROT_DIM = 16
ROPE_THETA = 500000.0
D_FF = 5632
PAST_LEN = 16384
PAGE_SIZE = 128
NORM_EPS = 1e-6
LNX_EPS = 64e-5
SUBLN_EPS = 1e-5

LANES = 128
SUBLANES = 8
CHUNK = 64
VMEM_LIMIT = 60 * 1024 * 1024

NT_DIMS = (((1,), (1,)), ((), ()))
TN_DIMS = (((0,), (0,)), ((), ()))


def _params(*sem):
    return pltpu.CompilerParams(dimension_semantics=sem, vmem_limit_bytes=VMEM_LIMIT)


def _dot(a, b, dims=None):
    if dims is None:
        return jnp.dot(a, b, preferred_element_type=F32)
    return lax.dot_general(a, b, dims, preferred_element_type=F32)


def _rms(x, g, eps):
    return x * lax.rsqrt(jnp.mean(x * x, axis=-1, keepdims=True) + eps) * g


def _sigmoid(x):
    return 1.0 / (1.0 + jnp.exp(-x))


def _group_matrix(scale):
    r = lax.broadcasted_iota(jnp.int32, (LANES, LANES), 0) // RWKV_N
    c = lax.broadcasted_iota(jnp.int32, (LANES, LANES), 1) // RWKV_N
    return jnp.where(r == c, scale, 0.0).astype(BF16)


def _group_sum(x, gm):
    return _dot(x.astype(BF16), gm)


FFN_TM = 1024
FFN_TF = 256


FFN_RIDERS = 16


def _ffn_kernel(x_ref, xs_ref, g_ref, w1_ref, w3_ref, w2_ref, o_ref, os_ref, h_ref):
    i = pl.program_id(0)
    f = pl.program_id(1)
    tm = x_ref.shape[0]
    ns = xs_ref.shape[0]

    @pl.when(f == 0)
    def _():
        h_ref[:tm, :] = _rms(x_ref[...], g_ref[...], NORM_EPS).astype(BF16)
        o_ref[...] = x_ref[...]

    @pl.when(jnp.logical_and(f == 0, i == 0))
    def _():
        h_ref[tm:, :] = jnp.zeros((FFN_RIDERS, h_ref.shape[1]), BF16)
        h_ref[tm:tm + ns, :] = _rms(xs_ref[...], g_ref[...], NORM_EPS).astype(BF16)
        os_ref[...] = xs_ref[...]

    def half_swiglu(h):
        gate = _dot(h, w1_ref[...].astype(BF16))
        up = _dot(h, w3_ref[...].astype(BF16))
        act = (0.5 * (gate * _sigmoid(gate) * up)).astype(BF16)
        return _dot(act, w2_ref[...].astype(BF16))

    @pl.when(i == 0)
    def _():
        y = half_swiglu(h_ref[...])
        o_ref[...] += y[:tm]
        os_ref[...] += y[tm:tm + ns]

    @pl.when(i > 0)
    def _():
        o_ref[...] += half_swiglu(h_ref[:tm, :])


def ffn(x, xs, g, w13, w2, layer, half):
    m, d = x.shape
    ns = xs.shape[0]
    assert ns <= FFN_RIDERS
    tm = min(FFN_TM, m)
    tf = FFN_TF
    nf = D_FF // tf
    return pl.pallas_call(
        _ffn_kernel,
        grid=(m // tm, nf),
        in_specs=[
            pl.BlockSpec((tm, d), lambda i, f: (i, 0)),
            pl.BlockSpec((ns, d), lambda i, f: (0, 0)),
            pl.BlockSpec((1, d), lambda i, f: (0, 0)),
            pl.BlockSpec((None, None, d, tf), lambda i, f: (layer, half, 0, f)),
            pl.BlockSpec((None, None, d, tf), lambda i, f: (layer, half, 0, f + nf)),
            pl.BlockSpec((None, None, tf, d), lambda i, f: (layer, half, f, 0)),
        ],
        out_specs=[pl.BlockSpec((tm, d), lambda i, f: (i, 0)), pl.BlockSpec((ns, d), lambda i, f: (0, 0))],
        out_shape=[jax.ShapeDtypeStruct((m, d), F32), jax.ShapeDtypeStruct((ns, d), F32)],
        scratch_shapes=[pltpu.VMEM((tm + FFN_RIDERS, d), BF16)],
        compiler_params=_params("arbitrary", "arbitrary"),
        name="ffn",
    )(x, xs, g.reshape(1, d), w13, w13, w2)


def _norm_mm_kernel(*refs, qk_epilogue, mxu_copy_scale, f32_layout):
    refs = list(refs)
    ob_ref = refs.pop() if mxu_copy_scale is not None else None
    o_ref = refs.pop() if f32_layout is not None else None
    x_ref, g_ref, w_ref = refs[:3]

    def emit(j, val):
        cols = slice(j * LANES, (j + 1) * LANES)
        if f32_layout == "rows":
            o_ref[:, cols] = val
        elif f32_layout == "transposed":
            o_ref[0, cols, :] = val.T
        elif f32_layout == "heads":
            o_ref[0, j] = val
        if ob_ref is not None:
            ob_ref[:, cols] = (val * mxu_copy_scale).astype(BF16)

    y = _dot(_rms(x_ref[...], g_ref[...], NORM_EPS).astype(BF16), w_ref[...])
    if not qk_epilogue:
        for j in range(y.shape[1] // LANES):
            emit(j, y[:, j * LANES:(j + 1) * LANES])
        return
    gh_ref, cos_ref, s1_ref, s2_ref = refs[3:7]
    gm = _group_matrix(1.0 / DIFF_DH)
    gh, cos, s1, s2 = gh_ref[...], cos_ref[...], s1_ref[...], s2_ref[...]
    half = ROT_DIM // 2
    for j in range(y.shape[1] // LANES):
        blk = y[:, j * LANES:(j + 1) * LANES]
        nb = blk * lax.rsqrt(_group_sum(blk * blk, gm) + NORM_EPS) * gh
        emit(j, nb * cos + pltpu.roll(nb, LANES - half, 1) * s1 + pltpu.roll(nb, half, 1) * s2)


def norm_mm(x, g, w, tm, rows_per_seq, rope=None, head_gain=None, mxu_copy_scale=None, f32_layout="rows"):
    m, d = x.shape
    n = w.shape[1]
    qk = rope is not None
    tiles_per_seq = rows_per_seq // tm
    seqs = m // rows_per_seq
    fixed = lambda i: (0, 0)
    in_specs = [pl.BlockSpec((tm, d), lambda i: (i, 0)), pl.BlockSpec((1, d), fixed), pl.BlockSpec((d, n), fixed)]
    args = [x, g.reshape(1, d), w]
    if qk:
        in_specs.append(pl.BlockSpec((1, LANES), fixed))
        args.append(jnp.tile(head_gain.reshape(1, DIFF_DH), (1, 2)))
        for t in rope:
            in_specs.append(pl.BlockSpec((tm, LANES), lambda i: (i % tiles_per_seq, 0)))
            args.append(t)
    out_specs, out_shape = [], []
    if f32_layout == "rows":
        out_specs.append(pl.BlockSpec((tm, n), lambda i: (i, 0)))
        out_shape.append(jax.ShapeDtypeStruct((m, n), F32))
    elif f32_layout == "transposed":
        out_specs.append(pl.BlockSpec((1, n, tm), lambda i: (i // tiles_per_seq, 0, i % tiles_per_seq)))
        out_shape.append(jax.ShapeDtypeStruct((seqs, n, rows_per_seq), F32))
    elif f32_layout == "heads":
        out_specs.append(pl.BlockSpec((1, n // LANES, tm, LANES),
                                      lambda i: (i // tiles_per_seq, 0, i % tiles_per_seq, 0)))
        out_shape.append(jax.ShapeDtypeStruct((seqs, n // LANES, rows_per_seq, LANES), F32))
    if mxu_copy_scale is not None:
        out_specs.append(pl.BlockSpec((tm, n), lambda i: (i, 0)))
        out_shape.append(jax.ShapeDtypeStruct((m, n), BF16))
    return pl.pallas_call(
        functools.partial(_norm_mm_kernel, qk_epilogue=qk, mxu_copy_scale=mxu_copy_scale, f32_layout=f32_layout),
        grid=(m // tm,),
        in_specs=in_specs,
        out_specs=out_specs,
        out_shape=out_shape,
        compiler_params=_params("parallel"),
        name="norm_mm_qk" if qk else "norm_mm",
    )(*args)


def rope_tables(pos):
    half = ROT_DIM // 2
    inv = ROPE_THETA ** (-jnp.arange(0, ROT_DIM, 2, dtype=F32) / ROT_DIM)
    ang = pos.astype(F32)[:, None] * inv[None, :]
    cos, sin = jnp.cos(ang), jnp.sin(ang)
    t = pos.shape[0]
    rest = DIFF_DH - ROT_DIM
    c64 = jnp.concatenate([cos, cos, jnp.ones((t, rest), F32)], axis=1)
    s1_64 = jnp.concatenate([-sin, jnp.zeros((t, DIFF_DH - half), F32)], axis=1)
    s2_64 = jnp.concatenate([jnp.zeros((t, half), F32), sin, jnp.zeros((t, rest), F32)], axis=1)
    return tuple(jnp.tile(z, (1, 2)) for z in (c64, s1_64, s2_64))


def _mem_kv_kernel(x_ref, g_ref, w_ref, gk_ref, k_ref, v_ref):
    h = _rms(x_ref[...], g_ref[...], NORM_EPS).astype(BF16)
    y = _dot(h, w_ref[...])
    gk = gk_ref[...]
    for j in range(MEM_HEADS):
        blk = y[:, j * MEM_DH:(j + 1) * MEM_DH]
        k_ref[:, j * MEM_DH:(j + 1) * MEM_DH] = _rms(blk, gk, NORM_EPS)
    v_ref[...] = y[:, MEM_W:]


def mem_kv(mem, g, w, gk):
    m, d = mem.shape
    return pl.pallas_call(
        _mem_kv_kernel,
        grid=(1,),
        in_specs=[
            pl.BlockSpec((m, d), lambda i: (0, 0)),
            pl.BlockSpec((1, d), lambda i: (0, 0)),
            pl.BlockSpec((d, 2 * MEM_W), lambda i: (0, 0)),
            pl.BlockSpec((1, MEM_DH), lambda i: (0, 0)),
        ],
        out_specs=[pl.BlockSpec((m, MEM_W), lambda i: (0, 0))] * 2,
        out_shape=[jax.ShapeDtypeStruct((m, MEM_W), F32)] * 2,
        compiler_params=_params("arbitrary"),
        name="mem_kv",
    )(mem, g.reshape(1, d), w, gk.reshape(1, MEM_DH))


def _mem_attn_kernel(q_ref, k_ref, v_ref, gq_ref, o_ref):
    q = q_ref[0]
    rows = q.shape[0]
    if rows < 8:
        q = jnp.broadcast_to(q, (8, q.shape[1]))
    k = k_ref[0]
    v = v_ref[0]
    gq = gq_ref[...]
    for h in range(MEM_HEADS):
        sl = slice(h * MEM_DH, (h + 1) * MEM_DH)
        qh = _rms(q[:, sl], gq, NORM_EPS).astype(BF16)
        s = _dot(qh, k[:, sl].astype(BF16), NT_DIMS) * (MEM_DH ** -0.5)
        p = jnp.exp(s - jnp.max(s, axis=-1, keepdims=True))
        o = _dot(p.astype(BF16), v[:, sl].astype(BF16)) / jnp.sum(p, axis=-1, keepdims=True)
        o_ref[0, :, sl] = o[:rows].astype(o_ref.dtype)


def mem_attn(proj, q_col_block, mk, mv, gq, tq):
    b, t, _ = proj.shape
    return pl.pallas_call(
        _mem_attn_kernel,
        grid=(b, t // tq),
        in_specs=[
            pl.BlockSpec((1, tq, MEM_W), lambda i, j: (i, j, q_col_block)),
            pl.BlockSpec((1, MEM_TOKENS, MEM_W), lambda i, j: (i, 0, 0)),
            pl.BlockSpec((1, MEM_TOKENS, MEM_W), lambda i, j: (i, 0, 0)),
            pl.BlockSpec((1, MEM_DH), lambda i, j: (0, 0)),
        ],
        out_specs=pl.BlockSpec((1, tq, MEM_W), lambda i, j: (i, j, 0)),
        out_shape=jax.ShapeDtypeStruct((b, t, MEM_W), BF16),
        compiler_params=_params("parallel", "arbitrary"),
        name="mem_attn",
    )(proj, mk, mv, gq.reshape(1, MEM_DH))


def _out_mm_kernel(x_ref, a_ref, b_ref, wa_ref, wb_ref, o_ref):
    o_ref[...] = x_ref[...] + _dot(a_ref[...], wa_ref[...]) + _dot(b_ref[...], wb_ref[...])


def out_mm(x, a, b, w, layer, tm):
    m, d = x.shape
    tn = d
    ka, kb = a.shape[1], b.shape[1]
    kb_blocks = ka // kb
    return pl.pallas_call(
        _out_mm_kernel,
        grid=(m // tm, d // tn),
        in_specs=[
            pl.BlockSpec((tm, tn), lambda i, j: (i, j)),
            pl.BlockSpec((tm, ka), lambda i, j: (i, 0)),
            pl.BlockSpec((tm, kb), lambda i, j: (i, 0)),
            pl.BlockSpec((None, ka, tn), lambda i, j: (layer, 0, j)),
            pl.BlockSpec((None, kb, tn), lambda i, j: (layer, kb_blocks, j)),
        ],
        out_specs=pl.BlockSpec((tm, tn), lambda i, j: (i, j)),
        out_shape=jax.ShapeDtypeStruct((m, d), F32),
        compiler_params=_params("parallel", "arbitrary"),
        name="out_mm",
    )(x, a, b, w, w)


def _a_prep_kernel(x_ref, xb_ref, sp_ref, g_ref, mu_ref, xl_ref, cat_ref, mix_ref, *, tiles_per_seq):
    g = g_ref[...]
    xn = _rms(x_ref[...], g, NORM_EPS)
    if tiles_per_seq is None:
        xp = sp_ref[...]
    else:
        before = _rms(xb_ref[...], g, NORM_EPS)[SUBLANES - 1:SUBLANES, :]
        first = pl.program_id(0) % tiles_per_seq == 0
        before = jnp.where(first, sp_ref[0], before)
        row = lax.broadcasted_iota(jnp.int32, xn.shape, 0)
        xp = jnp.where(row == 0, before, pltpu.roll(xn, 1, 0))
    xx = xp - xn
    d = xn.shape[1]
    xl_ref[0] = xn[xn.shape[0] - SUBLANES:, :]
    cat_ref[:, :d] = xn.astype(BF16)
    cat_ref[:, d:] = xx.astype(BF16)
    for i in range(3):
        mix_ref[:, i * d:(i + 1) * d] = (xn + xx * mu_ref[i:i + 1, :]).astype(BF16)


def a_prep(x, shift_prev, g, mu, tm, rows_per_seq):
    m, d = x.shape
    if rows_per_seq == 1:
        assert m == tm == SUBLANES
        tiles_per_seq, groups = None, 1
        sp, sp_spec = shift_prev, pl.BlockSpec((tm, d), lambda i: (i, 0))
        last_map = lambda i: (0, 0, 0)
        before_map = lambda i: (0, 0)
    else:
        tiles_per_seq, groups = rows_per_seq // tm, m // rows_per_seq
        sp = shift_prev.reshape(shift_prev.shape[0], 1, d)
        sp_spec = pl.BlockSpec((1, 1, d), lambda i: (i // tiles_per_seq, 0, 0))
        last_map = lambda i: (i // tiles_per_seq, 0, 0)
        before_map = lambda i: (jnp.maximum(i * (tm // SUBLANES) - 1, 0), 0)
    return pl.pallas_call(
        functools.partial(_a_prep_kernel, tiles_per_seq=tiles_per_seq),
        grid=(m // tm,),
        in_specs=[
            pl.BlockSpec((tm, d), lambda i: (i, 0)),
            pl.BlockSpec((SUBLANES, d), before_map),
            sp_spec,
            pl.BlockSpec((1, d), lambda i: (0, 0)),
            pl.BlockSpec((3, d), lambda i: (0, 0)),
        ],
        out_specs=[
            pl.BlockSpec((1, SUBLANES, d), last_map),
            pl.BlockSpec((tm, 2 * d), lambda i: (i, 0)),
            pl.BlockSpec((tm, 3 * d), lambda i: (i, 0)),
        ],
        out_shape=[
            jax.ShapeDtypeStruct((groups, SUBLANES, d), F32),
            jax.ShapeDtypeStruct((m, 2 * d), BF16),
            jax.ShapeDtypeStruct((m, 3 * d), BF16),
        ],
        compiler_params=_params("arbitrary"),
        name="a_prep",
    )(x, x, sp, g.reshape(1, d), mu)


def _mm_kernel(a_ref, w_ref, o_ref):
    o_ref[...] = _dot(a_ref[...], w_ref[...])


def mm(a, w, layer, tm, tn=1024):
    m, k = a.shape
    n = w.shape[2]
    return pl.pallas_call(
        _mm_kernel,
        grid=(m // tm, n // tn),
        in_specs=[pl.BlockSpec((tm, k), lambda i, j: (i, 0)),
                  pl.BlockSpec((None, k, tn), lambda i, j: (layer, 0, j))],
        out_specs=pl.BlockSpec((tm, tn), lambda i, j: (i, j)),
        out_shape=jax.ShapeDtypeStruct((m, n), F32),
        compiler_params=_params("parallel", "arbitrary"),
        name="mm",
    )(a, w)


def _a_mix_kernel(k_ref, mix_ref, w1_ref, w2_ref, a1_ref, a2_ref, g1_ref, g2_ref, w0_ref, a0_ref, kk_ref,
                  ka_ref, ld_ref, kp_ref, kn_ref, b_ref, g_ref):
    d = D_MODEL
    xw, xa, xg = mix_ref[:, :d], mix_ref[:, d:2 * d], mix_ref[:, 2 * d:]
    wl = _dot(jnp.tanh(_dot(xw, w1_ref[...])).astype(BF16), w2_ref[...]) + w0_ref[...]
    ld_ref[...] = -math.exp(-0.5) * _sigmoid(wl)
    a = _sigmoid(_dot(_dot(xa, a1_ref[...]).astype(BF16), a2_ref[...]) + a0_ref[...])
    g_ref[...] = _dot(_sigmoid(_dot(xg, g1_ref[...])).astype(BF16), g2_ref[...])
    k = k_ref[...]
    kp_ref[...] = k * (1.0 + (a - 1.0) * ka_ref[...])
    kraw = k * kk_ref[...]
    gm = _group_matrix(1.0)
    for j in range(RWKV_W // LANES):
        sl = slice(j * LANES, (j + 1) * LANES)
        blk = kraw[:, sl]
        kn = blk / jnp.maximum(jnp.sqrt(_group_sum(blk * blk, gm)), 1e-12)
        kn_ref[:, sl] = kn
        b_ref[:, sl] = kn * a[:, sl]


def a_mix(proj, mix, lw, tm):
    m = proj.shape[0]
    row = lambda i: (i, 0)
    fixed = lambda i: (0, 0)
    vec = pl.BlockSpec((1, RWKV_W), fixed)
    out = jax.ShapeDtypeStruct((m, RWKV_W), F32)
    return pl.pallas_call(
        _a_mix_kernel,
        grid=(m // tm,),
        in_specs=[
            pl.BlockSpec((tm, RWKV_W), lambda i: (i, 1)),
            pl.BlockSpec((tm, 3 * D_MODEL), row),
            pl.BlockSpec((D_MODEL, LORA_PAD), fixed), pl.BlockSpec((LORA_PAD, RWKV_W), fixed),
            pl.BlockSpec((D_MODEL, LORA_PAD), fixed), pl.BlockSpec((LORA_PAD, RWKV_W), fixed),
            pl.BlockSpec((D_MODEL, LORA_G), fixed), pl.BlockSpec((LORA_G, RWKV_W), fixed),
            vec, vec, vec, vec,
        ],
        out_specs=[pl.BlockSpec((tm, RWKV_W), row)] * 5,
        out_shape=[out] * 5,
        compiler_params=_params("parallel"),
        name="a_mix",
    )(proj, mix, lw["w1"], lw["w2"], lw["a1"], lw["a2"], lw["g1"], lw["g2"],
      lw["w0"], lw["a0"], lw["k_k"], lw["k_a"])


def _wkv_masks(c):
    c2 = 2 * c
    ri = lax.broadcasted_iota(jnp.int32, (c2, c2), 0)
    ci = lax.broadcasted_iota(jnp.int32, (c2, c2), 1)
    same = (ri // c) == (ci // c)
    masks = dict(
        in_head0=lax.broadcasted_iota(jnp.int32, (c, LANES), 1) < RWKV_N,
        tri=jnp.where(lax.broadcasted_iota(jnp.int32, (c, c), 0) >= lax.broadcasted_iota(jnp.int32, (c, c), 1),
                      1.0, 0.0).astype(BF16),
        strict=jnp.logical_and(same, (ri % c) > (ci % c)),
        incl=jnp.logical_and(same, (ri % c) >= (ci % c)),
        eye=jnp.where(ri == ci, 1.0, 0.0),
        diag8=(ri // 8) == (ci // 8),
        lower_left=[],
    )
    size = 8
    while size < c:
        masks["lower_left"].append(jnp.logical_and(
            (ri // (2 * size)) == (ci // (2 * size)),
            jnp.logical_and((ri // size) % 2 == 1, (ci // size) % 2 == 0)))
        size *= 2
    return masks


def _each(f, *lists):
    return [f(*args) for args in zip(*lists)]


def _wkv_chunk(s, r, ld, k, v, kn, b, mk):
    c = r[0].shape[0]
    c2 = 2 * c
    in_head0 = mk["in_head0"]
    tri = mk["tri"]

    def stack(z):
        return jnp.concatenate([jnp.where(in_head0, z, 0.0), jnp.where(in_head0, 0.0, z)], axis=0)

    def bdot(x, y):
        return _dot(x.astype(BF16), y.astype(BF16))

    p1 = _each(lambda z: z.astype(BF16), ld)
    rem = _each(lambda z, p: z - p.astype(F32), ld, p1)
    p2 = _each(lambda z: z.astype(BF16), rem)
    p3 = _each(lambda z, p: (z - p.astype(F32)).astype(BF16), rem, p2)
    cum = _each(lambda a1, a2, a3: _dot(tri, a1) + _dot(tri, a2) + _dot(tri, a3), p1, p2, p3)
    eg = _each(jnp.exp, cum)
    einv = _each(lambda z: jnp.exp(-z), cum)
    at_s = _each(lambda n, z, d: stack(-n * jnp.exp(z - d)).astype(BF16), kn, cum, ld)
    rt_s = _each(lambda x, e: stack(x * e).astype(BF16), r, eg)
    bt = _each(lambda x, e: x * e, b, einv)
    kt = _each(lambda x, e: x * e, k, einv)
    v_s = _each(lambda x: stack(x).astype(BF16), v)

    gmat = _each(lambda a, x, y, z: _dot(jnp.concatenate([a, x], axis=0),
                                         jnp.concatenate([y, y, z, z], axis=0).astype(BF16), NT_DIMS),
                 at_s, rt_s, bt, kt)
    a_ab = _each(lambda g: jnp.where(mk["strict"], g[:c2, :c2], 0.0), gmat)
    a_ak = _each(lambda g: jnp.where(mk["strict"], g[:c2, c2:], 0.0).astype(BF16), gmat)
    a_rr = _each(lambda g: jnp.concatenate([jnp.where(mk["incl"], g[c2:, :c2], 0.0),
                                            jnp.where(mk["incl"], g[c2:, c2:], 0.0)], axis=1).astype(BF16), gmat)

    ad = _each(lambda a: jnp.where(mk["diag8"], a, 0.0), a_ab)
    ad2 = _each(bdot, ad, ad)
    ad4 = _each(bdot, ad2, ad2)
    inv = _each(lambda a: mk["eye"] + a, ad)
    inv = _each(lambda x, y: x + bdot(x, y), inv, ad2)
    inv = _each(lambda x, y: x + bdot(x, y), inv, ad4)
    for lower_left in mk["lower_left"]:
        half = _each(lambda x, a: bdot(x, jnp.where(lower_left, a, 0.0)), inv, a_ab)
        inv = _each(lambda x, y: x + bdot(y, x), inv, half)

    s_b = _each(lambda z: z.astype(BF16), s)
    rhs = _each(lambda a, x, y, z: _dot(a, x) + _dot(y, z, NT_DIMS), a_ak, v_s, at_s, s_b)
    u = _each(lambda x, y: bdot(x, y).astype(BF16), inv, rhs)
    uv = _each(lambda x, y: jnp.concatenate([x, y], axis=0), u, v_s)
    y_s = _each(lambda x, z, a, w: _dot(x, z, NT_DIMS) + _dot(a, w), rt_s, s_b, a_rr, uv)
    ds = _each(lambda w, x, y: _dot(w, jnp.concatenate([stack(x), stack(y)], axis=0).astype(BF16), TN_DIMS),
               uv, bt, kt)
    y = _each(lambda z: z[:c] + z[c:], y_s)
    s_new = _each(lambda z, dz, e: (z + dz) * e[c - 1:c, :], s, ds, eg)
    return y, s_new


def _wkv_kernel(r_ref, ld_ref, k_ref, v_ref, kn_ref, b_ref, y_ref, s_ref, st_ref, *, chunks, pairs):
    @pl.when(pl.program_id(2) == 0)
    def _():
        st_ref[...] = jnp.zeros_like(st_ref)

    mk = _wkv_masks(CHUNK)
    s = [st_ref[p] for p in range(pairs)]
    for i in range(chunks):
        rows = slice(i * CHUNK, (i + 1) * CHUNK)
        cut = lambda ref: [ref[0, rows, p * LANES:(p + 1) * LANES] for p in range(pairs)]
        y, s = _wkv_chunk(s, cut(r_ref), cut(ld_ref), cut(k_ref), cut(v_ref), cut(kn_ref), cut(b_ref), mk)
        for p in range(pairs):
            y_ref[0, rows, p * LANES:(p + 1) * LANES] = y[p]
    for p in range(pairs):
        st_ref[p] = s[p]

    @pl.when(pl.program_id(2) == pl.num_programs(2) - 1)
    def _():
        s_ref[0] = st_ref[...]


WKV_PAIRS = 12
WKV_CHUNKS = 2


def wkv_scan(proj, ld, kp, kn, b):
    bsz, t, _ = proj.shape
    groups = RWKV_W // LANES // WKV_PAIRS
    tb = WKV_CHUNKS * CHUNK
    width = WKV_PAIRS * LANES
    blk = lambda off: pl.BlockSpec((1, tb, width), lambda i, p, c: (i, c, p + off))
    return pl.pallas_call(
        functools.partial(_wkv_kernel, chunks=WKV_CHUNKS, pairs=WKV_PAIRS),
        grid=(bsz, groups, t // tb),
        in_specs=[blk(0), blk(0), blk(0), blk(2 * groups), blk(0), blk(0)],
        out_specs=[
            pl.BlockSpec((1, tb, width), lambda i, p, c: (i, c, p)),
            pl.BlockSpec((1, WKV_PAIRS, LANES, LANES), lambda i, p, c: (i, p, 0, 0)),
        ],
        out_shape=[
            jax.ShapeDtypeStruct((bsz, t, RWKV_W), F32),
            jax.ShapeDtypeStruct((bsz, RWKV_W // LANES, LANES, LANES), F32),
        ],
        scratch_shapes=[pltpu.VMEM((WKV_PAIRS, LANES, LANES), F32)],
        compiler_params=_params("parallel", "parallel", "arbitrary"),
        name="wkv_scan",
    )(proj, ld, kp, proj, kn, b)


def _wkv_step_kernel(s_ref, r_ref, ld_ref, k_ref, v_ref, kn_ref, b_ref, y_ref, so_ref):
    s = s_ref[0]
    sa = jnp.sum(s * (-kn_ref[0]), axis=-1, keepdims=True)
    s = s * jnp.exp(ld_ref[0]) + sa * b_ref[0] + v_ref[0] * k_ref[0]
    so_ref[0] = s
    y_ref[0] = jnp.sum(s * r_ref[0], axis=-1, keepdims=True)


def wkv_step(s0, r, ld, k, v, kn, b):
    bsz = s0.shape[0]
    h, n = RWKV_HEADS, RWKV_N
    st = pl.BlockSpec((1, h, n, n), lambda i: (i, 0, 0, 0))
    rw = pl.BlockSpec((1, h, 1, n), lambda i: (i, 0, 0, 0))
    cl = pl.BlockSpec((1, h, n, 1), lambda i: (i, 0, 0, 0))
    return pl.pallas_call(
        _wkv_step_kernel,
        grid=(bsz,),
        in_specs=[st, rw, rw, rw, cl, rw, rw],
        out_specs=[cl, st],
        out_shape=[jax.ShapeDtypeStruct((bsz, h, n, 1), F32), jax.ShapeDtypeStruct((bsz, h, n, n), F32)],
        compiler_params=_params("parallel"),
        name="wkv_step",
    )(s0, r, ld, k, v, kn, b)


def _a_post_kernel(y_ref, r_ref, kp_ref, v_ref, g_ref, lw_ref, lb_ref, rk_ref, o_ref):
    gsum = _group_matrix(1.0)
    gmean = _group_matrix(1.0 / RWKV_N)
    for j in range(RWKV_W // LANES):
        sl = slice(j * LANES, (j + 1) * LANES)
        y = y_ref[:, sl]
        cen = y - _group_sum(y, gmean)
        yn = cen * lax.rsqrt(_group_sum(cen * cen, gmean) + LNX_EPS) * lw_ref[:, sl] + lb_ref[:, sl]
        bonus = _group_sum(r_ref[:, sl] * kp_ref[:, sl] * rk_ref[:, sl], gsum) * v_ref[:, sl]
        o_ref[:, sl] = ((yn + bonus) * g_ref[:, sl]).astype(o_ref.dtype)


def a_post(y, proj, kp, g, lnx_w, lnx_b, r_k, tm):
    m = y.shape[0]
    row = lambda i: (i, 0)
    vec = pl.BlockSpec((1, RWKV_W), lambda i: (0, 0))
    tile = pl.BlockSpec((tm, RWKV_W), row)
    return pl.pallas_call(
        _a_post_kernel,
        grid=(m // tm,),
        in_specs=[tile, pl.BlockSpec((tm, RWKV_W), lambda i: (i, 0)), tile,
                  pl.BlockSpec((tm, RWKV_W), lambda i: (i, 2)), tile, vec, vec, vec],
        out_specs=tile,
        out_shape=jax.ShapeDtypeStruct((m, RWKV_W), BF16),
        compiler_params=_params("parallel"),
        name="a_post",
    )(y, proj, kp, proj, g, lnx_w.reshape(1, RWKV_W), lnx_b.reshape(1, RWKV_W), r_k.reshape(1, RWKV_W))


def _lambda(lam_ref, lam_init):
    lq = lam_ref[...]
    l1 = jnp.sum(lq[0:1] * lq[1:2], axis=-1, keepdims=True)
    l2 = jnp.sum(lq[2:3] * lq[3:4], axis=-1, keepdims=True)
    return jnp.exp(l1) - jnp.exp(l2) + lam_init


ATTN_HEADS = 2
ATTN_TQ = 512
LOG2E = 1.4426950408889634
Q_SCALE = DIFF_DH ** -0.5 * LOG2E


def _diff_attn_kernel(qi_ref, ki_ref, q_ref, k_ref, v_ref, lam_ref, gs_ref, o_ref, m_ref, l_ref, acc_ref, *,
                      tq, lam_init):
    qi = qi_ref[pl.program_id(2)]
    ki = ki_ref[pl.program_id(2)]

    @pl.when(ki == 0)
    def _():
        m_ref[...] = jnp.full_like(m_ref, -jnp.inf)
        l_ref[...] = jnp.zeros_like(l_ref)
        acc_ref[...] = jnp.zeros_like(acc_ref)

    def step(diagonal):
        lane = lax.broadcasted_iota(jnp.int32, (tq, DIFF_DV), 1)
        if diagonal:
            visible = (lax.broadcasted_iota(jnp.int32, (tq, tq), 1)
                       <= lax.broadcasted_iota(jnp.int32, (tq, tq), 0))
        for h in range(ATTN_HEADS):
            cols = slice(h * DIFF_DV, (h + 1) * DIFF_DV)
            q = q_ref[0, :, cols]
            kb = k_ref[0, :, cols]
            v1 = jnp.concatenate([v_ref[0, :, cols], jnp.ones((tq, LANES), BF16)], axis=1)
            for c in range(2):
                idx = 2 * h + c
                in_comp = (lane < DIFF_DH) if c == 0 else (lane >= DIFF_DH)
                s = _dot(jnp.where(in_comp, q, jnp.zeros_like(q)), kb, NT_DIMS)
                if diagonal:
                    s = jnp.where(visible, s, -jnp.inf)
                m_old = m_ref[idx]
                m_new = jnp.maximum(m_old, jnp.max(s, axis=-1, keepdims=True))
                alpha = jnp.exp2(m_old - m_new)
                p = jnp.exp2(s - jnp.concatenate([m_new] * (tq // LANES), axis=1))
                pv = _dot(p.astype(BF16), v1)
                l_ref[idx] = alpha * l_ref[idx] + pv[:, DIFF_DV:]
                acc_ref[idx] = alpha * acc_ref[idx] + pv[:, :DIFF_DV]
                m_ref[idx] = m_new

    @pl.when(ki < qi)
    def _():
        step(False)

    @pl.when(ki == qi)
    def _():
        step(True)
        lam = _lambda(lam_ref, lam_init)
        for h in range(ATTN_HEADS):
            o = acc_ref[2 * h] / l_ref[2 * h] - lam * (acc_ref[2 * h + 1] / l_ref[2 * h + 1])
            o_ref[0, :, h * DIFF_DV:(h + 1) * DIFF_DV] = (
                _rms(o, gs_ref[...], SUBLN_EPS) * (1.0 - lam_init)).astype(o_ref.dtype)


def diff_attn_prompt(q, k, v, lam_p, subln, lam_init):
    b, t, _ = q.shape
    tq = ATTN_TQ
    width = ATTN_HEADS * DIFF_DV
    chains = 2 * ATTN_HEADS
    blocks = t // tq
    pairs = [(qi, ki) for qi in range(blocks) for ki in range(qi + 1)]
    q_of = jnp.asarray([p[0] for p in pairs], jnp.int32)
    k_of = jnp.asarray([p[1] for p in pairs], jnp.int32)
    q_map = lambda i, h, p, q_of, k_of: (i, q_of[p], h)
    kv_map = lambda i, h, p, q_of, k_of: (i, k_of[p], h)
    fixed = lambda i, h, p, q_of, k_of: (0, 0)
    grid_spec = pltpu.PrefetchScalarGridSpec(
        num_scalar_prefetch=2,
        grid=(b, DIFF_HEADS // ATTN_HEADS, len(pairs)),
        in_specs=[
            pl.BlockSpec((1, tq, width), q_map),
            pl.BlockSpec((1, tq, width), kv_map),
            pl.BlockSpec((1, tq, width), kv_map),
            pl.BlockSpec((4, DIFF_DH), fixed),
            pl.BlockSpec((1, DIFF_DV), fixed),
        ],
        out_specs=pl.BlockSpec((1, tq, width), q_map),
        scratch_shapes=[pltpu.VMEM((chains, tq, LANES), F32), pltpu.VMEM((chains, tq, LANES), F32),
                        pltpu.VMEM((chains, tq, DIFF_DV), F32)],
    )
    return pl.pallas_call(
        functools.partial(_diff_attn_kernel, tq=tq, lam_init=lam_init),
        grid_spec=grid_spec,
        out_shape=jax.ShapeDtypeStruct((b, t, DIFF_W), BF16),
        compiler_params=_params("parallel", "parallel", "arbitrary"),
        name="diff_attn_prompt",
    )(q_of, k_of, q, k, v, lam_p, subln.reshape(1, DIFF_DV))


DEC_PAGES = 4
DEC_ROWS = 8


def _dec_attn_kernel(pt_ref, q_ref, kn_ref, vn_ref, lam_ref, gs_ref, *refs, lam_init):
    k_refs = refs[:DEC_PAGES]
    v_refs = refs[DEC_PAGES:2 * DEC_PAGES]
    o_ref, qb_ref, m_ref, l_ref, acc_ref = refs[2 * DEC_PAGES:]
    step = pl.program_id(1)
    row = lax.broadcasted_iota(jnp.int32, (DIFF_HEADS, DEC_ROWS, LANES), 1)

    def comp_rows(prod):
        s0 = jnp.sum(prod[:, :DIFF_DH, :], axis=1, keepdims=True)
        s1 = jnp.sum(prod[:, DIFF_DH:, :], axis=1, keepdims=True)
        return jnp.where(row == 0, s0, jnp.where(row == 1, s1, 0.0))

    @pl.when(step == 0)
    def _():
        q_col = q_ref[0] * (DIFF_DH ** -0.5)
        qb_ref[...] = jnp.broadcast_to(q_col, qb_ref.shape)
        m_ref[...] = comp_rows(q_col * kn_ref[0])
        l_ref[...] = jnp.ones_like(l_ref)
        acc_ref[...] = jnp.broadcast_to(vn_ref[0], acc_ref.shape)

    qb = qb_ref[...]
    for k_ref, v_ref in zip(k_refs, v_refs):
        s = comp_rows(k_ref[0] * qb)
        m_old = m_ref[...]
        m_new = jnp.maximum(m_old, jnp.max(s, axis=-1, keepdims=True))
        alpha = jnp.exp(m_old - m_new)
        p = jnp.exp(s - m_new)
        l_ref[...] = alpha * l_ref[...] + jnp.sum(p, axis=-1, keepdims=True)
        pv = lax.dot_general(p.astype(BF16), v_ref[0].astype(BF16), (((2,), (1,)), ((0,), (0,))),
                             preferred_element_type=F32)
        acc_ref[...] = alpha * acc_ref[...] + pv
        m_ref[...] = m_new

    @pl.when(step == pl.num_programs(1) - 1)
    def _():
        lam = _lambda(lam_ref, lam_init)
        w = acc_ref[...] / l_ref[...]
        o = w[:, 0:1, :] - lam * w[:, 1:2, :]
        o = o * lax.rsqrt(jnp.mean(o * o, axis=-1, keepdims=True) + SUBLN_EPS) * gs_ref[...] * (1.0 - lam_init)
        o_ref[0] = jnp.broadcast_to(o, o_ref.shape[1:])


def diff_attn_decode(q_col, k_col, v_new, cache_kt, cache_v, page_table, lam_p, subln, lam_init):
    b = q_col.shape[0]
    n_pages = page_table.shape[1]
    col = pl.BlockSpec((1, DIFF_HEADS, DIFF_DV, 1), lambda i, s, pt: (i, 0, 0, 0))
    state = pltpu.VMEM((DIFF_HEADS, DEC_ROWS, LANES), F32)

    def page_spec(j):
        return pl.BlockSpec((1, DIFF_HEADS, PAGE_SIZE, DIFF_DV), lambda i, s, pt: (pt[i, s * DEC_PAGES + j], 0, 0, 0))

    grid_spec = pltpu.PrefetchScalarGridSpec(
        num_scalar_prefetch=1,
        grid=(b, n_pages // DEC_PAGES),
        in_specs=[col, col,
                  pl.BlockSpec((1, DIFF_HEADS, 1, DIFF_DV), lambda i, s, pt: (i, 0, 0, 0)),
                  pl.BlockSpec((4, DIFF_DH), lambda i, s, pt: (0, 0)),
                  pl.BlockSpec((1, DIFF_DV), lambda i, s, pt: (0, 0))]
        + [page_spec(j) for j in range(DEC_PAGES)] * 2,
        out_specs=pl.BlockSpec((1, DIFF_HEADS, DEC_ROWS, DIFF_DV), lambda i, s, pt: (i, 0, 0, 0)),
        scratch_shapes=[pltpu.VMEM((DIFF_HEADS, DIFF_DV, PAGE_SIZE), F32), state, state, state],
    )
    return pl.pallas_call(
        functools.partial(_dec_attn_kernel, lam_init=lam_init),
        grid_spec=grid_spec,
        out_shape=jax.ShapeDtypeStruct((b, DIFF_HEADS, DEC_ROWS, DIFF_DV), F32),
        compiler_params=_params("parallel", "arbitrary"),
        name="diff_attn_decode",
    )(page_table, q_col, k_col, v_new, lam_p, subln.reshape(1, DIFF_DV),
      *([cache_kt] * DEC_PAGES), *([cache_v] * DEC_PAGES))


def _pad_lora(w_in, w_out):
    pad = LORA_PAD - w_in.shape[1]
    return (jnp.pad(w_in, ((0, 0), (0, pad))).astype(BF16), jnp.pad(w_out, ((0, pad), (0, 0))).astype(BF16))


def kernel(x_prompt, x_sample, mem_prompt, state_wkv, state_shift, cache_mem_k, cache_mem_v, cache_k, cache_v, page_table, ffn_norm, ffn_w13, ffn_w2, mix_norm, w_out, mem_norm, mem_w_kv, mem_q_norm, mem_k_norm, a_w_in, a_mu, a_w0, a_w1, a_w2, a_a0, a_a1, a_a2, a_g1, a_g2, a_k_k, a_k_a, a_r_k, a_lnx_w, a_lnx_b, kv_norm, kv_w, k_norm, b_w_in, b_q_norm, b_lam, b_subln):
    d = D_MODEL
    wout_b = w_out.astype(BF16)
    memw_b = mem_w_kv.astype(BF16)
    awin_b = a_w_in.astype(BF16)
    kvw_k, kvw_v = kv_w[:, :DIFF_W].astype(BF16), kv_w[:, DIFF_W:].astype(BF16)
    bq_b, bm_b = b_w_in[:, :, :DIFF_W].astype(BF16), b_w_in[:, :, DIFF_W:].astype(BF16)
    loras = []
    for i in range(N_A):
        w1, w2 = _pad_lora(a_w1[i], a_w2[i])
        a1, a2 = _pad_lora(a_a1[i], a_a2[i])
        loras.append(dict(w1=w1, w2=w2, a1=a1, a2=a2, g1=a_g1[i].astype(BF16), g2=a_g2[i].astype(BF16),
                          w0=a_w0[i].reshape(1, RWKV_W), a0=a_a0[i].reshape(1, RWKV_W),
                          k_k=a_k_k[i].reshape(1, RWKV_W), k_a=a_k_a[i].reshape(1, RWKV_W)))

    def run(x3, shift_prev, wkv0, mk, mv, pos, decode):
        bsz, t, _ = x3.shape
        m = bsz * t
        tm = min(512, m)
        x = x3.reshape(m, d)
        rope = rope_tables(jnp.broadcast_to(pos, (m,)) if decode else pos)
        rope_rows = m if decode else t
        shifts, states = [], []
        for i in range(N_A):
            x = yield x, i, 0
            x_last, cat, mix = a_prep(x, shift_prev[i], mix_norm[i], a_mu[i], tm, t)
            shifts.append(x_last[0] if decode else x_last[:, SUBLANES - 1, :])
            proj = mm(cat, awin_b, i, tm)
            ld, kp, kn, bvec, gate = a_mix(proj, mix, loras[i], min(256, m))
            proj3 = proj.reshape(bsz, t, A_IN)
            if decode:
                heads = lambda z: z.reshape(bsz, RWKV_HEADS, 1, RWKV_N)
                y, s_new = wkv_step(
                    wkv0[i], heads(proj[:, :RWKV_W]), heads(ld), heads(kp),
                    proj[:, 2 * RWKV_W:3 * RWKV_W].reshape(bsz, RWKV_HEADS, RWKV_N, 1), heads(kn), heads(bvec))
                y = y.reshape(m, RWKV_W)
            else:
                r3 = lambda z: z.reshape(bsz, t, RWKV_W)
                y, s_pairs = wkv_scan(proj3, r3(ld), r3(kp), r3(kn), r3(bvec))
                y = y.reshape(m, RWKV_W)
                s_new = jnp.stack([s_pairs[:, :, :RWKV_N, :RWKV_N], s_pairs[:, :, RWKV_N:, RWKV_N:]], axis=2)
                s_new = s_new.reshape(bsz, RWKV_HEADS, RWKV_N, RWKV_N)
            states.append(s_new)
            y_mix = a_post(y, proj, kp, gate, a_lnx_w[i], a_lnx_b[i], a_r_k[i], min(256, m))
            o_mem = mem_attn(proj3, 3 * RWKV_W // MEM_W, mk[i], mv[i], mem_q_norm[i], min(512, t))
            x = out_mm(x, y_mix, o_mem.reshape(m, MEM_W), wout_b, i, tm)
            x = yield x, i, 1
        if decode:
            k_rows, = norm_mm(x, kv_norm, kvw_k, tm, rope_rows, rope, k_norm)
            v_rows, = norm_mm(x, kv_norm, kvw_v, tm, rope_rows)
            k_col = k_rows.reshape(bsz, DIFF_HEADS, DIFF_DV, 1)
            v_row = v_rows.reshape(bsz, DIFF_HEADS, 1, DIFF_DV)
            k_out = k_rows.reshape(bsz, t, DIFF_HEADS, 2, DIFF_DH)
            v_out = v_rows.reshape(bsz, t, DIFF_HEADS, DIFF_DV)
        else:
            k_t, k_b = norm_mm(x, kv_norm, kvw_k, tm, rope_rows, rope, k_norm, mxu_copy_scale=1.0,
                               f32_layout="transposed")
            v_h, v_b = norm_mm(x, kv_norm, kvw_v, tm, rope_rows, mxu_copy_scale=1.0, f32_layout="heads")
            k_b, v_b = k_b.reshape(bsz, t, DIFF_W), v_b.reshape(bsz, t, DIFF_W)
            k_out = jnp.transpose(k_t.reshape(bsz, DIFF_HEADS, 2, DIFF_DH, t), (0, 4, 1, 2, 3))
            v_out = jnp.transpose(v_h, (0, 2, 1, 3))
        for j in range(N_B):
            i = N_A + j
            lam_init = 0.8 - 0.6 * math.exp(-0.3 * i)
            x = yield x, i, 0
            q_mem = norm_mm(x, mix_norm[i], bm_b[j], tm, rope_rows)[0].reshape(bsz, t, MEM_W)
            if decode:
                q, = norm_mm(x, mix_norm[i], bq_b[j], tm, rope_rows, rope, b_q_norm[j])
                o = diff_attn_decode(q.reshape(bsz, DIFF_HEADS, DIFF_DV, 1), k_col, v_row, cache_kt, cache_vt,
                                     page_table, b_lam[j], b_subln[j], lam_init)
                o = o[:, :, 0, :].astype(BF16)
            else:
                q_b, = norm_mm(x, mix_norm[i], bq_b[j], tm, rope_rows, rope, b_q_norm[j], mxu_copy_scale=Q_SCALE,
                               f32_layout=None)
                o = diff_attn_prompt(q_b.reshape(bsz, t, DIFF_W), k_b, v_b, b_lam[j], b_subln[j], lam_init)
            o_mem = mem_attn(q_mem, 0, mk[i], mv[i], mem_q_norm[i], min(512, t))
            x = out_mm(x, o.reshape(m, DIFF_W), o_mem.reshape(m, MEM_W), wout_b, i, tm)
            x = yield x, i, 1
        return x.reshape(bsz, t, d), jnp.stack(states), jnp.stack(shifts), k_out, v_out

    n_pool = cache_k.shape[0]
    cache_kt = jnp.transpose(cache_k, (0, 2, 3, 4, 1)).reshape(n_pool, DIFF_HEADS, DIFF_DV, PAGE_SIZE)
    cache_vt = jnp.transpose(cache_v, (0, 2, 1, 3))

    bp, tp, _ = x_prompt.shape
    bs, ts, _ = x_sample.shape
    assert ts == 1, "the sample group is decoded one token per sequence"
    mem2 = mem_prompt.reshape(bp * MEM_TOKENS, d)
    mk_p, mv_p = [], []
    for i in range(DEPTH):
        mk_i, mv_i = mem_kv(mem2, mem_norm[i], memw_b[i], mem_k_norm[i])
        mk_p.append(mk_i.reshape(bp, MEM_TOKENS, MEM_W))
        mv_p.append(mv_i.reshape(bp, MEM_TOKENS, MEM_W))

    mk_s = cache_mem_k.reshape(DEPTH, bs, MEM_TOKENS, MEM_W)
    mv_s = cache_mem_v.reshape(DEPTH, bs, MEM_TOKENS, MEM_W)
    groups = [run(x_prompt, jnp.zeros((N_A, bp, d), F32), None, mk_p, mv_p, jnp.arange(tp), decode=False),
              run(x_sample, state_shift, state_wkv, mk_s, mv_s, jnp.full((1,), PAST_LEN, jnp.int32), decode=True)]
    requests = [next(gen) for gen in groups]
    results = [None, None]
    while results[0] is None:
        (x_p, layer, half), (x_s, layer_s, half_s) = requests
        assert (layer, half) == (layer_s, half_s)
        outs = ffn(x_p, x_s, ffn_norm[layer, half], ffn_w13, ffn_w2, layer, half)
        for n, gen in enumerate(groups):
            try:
                requests[n] = gen.send(outs[n])
            except StopIteration as done:
                results[n] = done.value
    assert results[1] is not None
    (y_p, wkv_p, shift_p, k_p, v_p), (y_s, wkv_s, shift_s, k_s, v_s) = results

    memshape = lambda zs: jnp.stack(zs).reshape(DEPTH, bp, MEM_TOKENS, MEM_HEADS, MEM_DH)
    return (y_p, y_s, wkv_p, shift_p, wkv_s, shift_s, k_p, v_p, k_s, v_s, memshape(mk_p), memshape(mv_p))
```

```python
import functools
import math

import jax
import jax.numpy as jnp
from jax import lax
from jax.experimental import pallas as pl
from jax.experimental.pallas import tpu as pltpu

F32 = jnp.float32
BF16 = jnp.bfloat16

D_MODEL = 2048
DEPTH = 4
N_A = 2
N_B = 2
MEM_TOKENS = 256
MEM_HEADS = 4
MEM_W = 512
MEM_DH = 128
RWKV_W = 1536
RWKV_N = 64
RWKV_HEADS = 24
LORA_PAD = 128
LORA_G = 256
A_IN = 3 * RWKV_W + MEM_W
DIFF_W = 1536
DIFF_DV = 128
DIFF_HEADS = 12
DIFF_DH = 64
ROT_DIM = 16
ROPE_THETA = 500000.0
D_FF = 5632
PAST_LEN = 16384
PAGE_SIZE = 128
NORM_EPS = 1e-6
LNX_EPS = 64e-5
SUBLN_EPS = 1e-5

LANES = 128
SUBLANES = 8
CHUNK = 64
VMEM_LIMIT = 60 * 1024 * 1024

NT_DIMS = (((1,), (1,)), ((), ()))
TN_DIMS = (((0,), (0,)), ((), ()))


def _params(*sem):
    return pltpu.CompilerParams(dimension_semantics=sem, vmem_limit_bytes=VMEM_LIMIT)


def _dot(a, b, dims=None):
    if dims is None:
        return jnp.dot(a, b, preferred_element_type=F32)
    return lax.dot_general(a, b, dims, preferred_element_type=F32)


def _rms(x, g, eps):
    return x * lax.rsqrt(jnp.mean(x * x, axis=-1, keepdims=True) + eps) * g


def _sigmoid(x):
    return 1.0 / (1.0 + jnp.exp(-x))


def _group_matrix(scale):
    r = lax.broadcasted_iota(jnp.int32, (LANES, LANES), 0) // RWKV_N
    c = lax.broadcasted_iota(jnp.int32, (LANES, LANES), 1) // RWKV_N
    return jnp.where(r == c, scale, 0.0).astype(BF16)


def _group_sum(x, gm):
    return _dot(x.astype(BF16), gm)


FFN_TM = 1024
FFN_TF = 256


FFN_RIDERS = 16


def _ffn_kernel(x_ref, xs_ref, g_ref, w1_ref, w3_ref, w2_ref, o_ref, os_ref, h_ref):
    i = pl.program_id(0)
    f = pl.program_id(1)
    tm = x_ref.shape[0]
    ns = xs_ref.shape[0]

    @pl.when(f == 0)
    def _():
        h_ref[:tm, :] = _rms(x_ref[...], g_ref[...], NORM_EPS).astype(BF16)
        o_ref[...] = x_ref[...]

    @pl.when(jnp.logical_and(f == 0, i == 0))
    def _():
        h_ref[tm:, :] = jnp.zeros((FFN_RIDERS, h_ref.shape[1]), BF16)
        h_ref[tm:tm + ns, :] = _rms(xs_ref[...], g_ref[...], NORM_EPS).astype(BF16)
        os_ref[...] = xs_ref[...]

    def half_swiglu(h):
        gate = _dot(h, w1_ref[...].astype(BF16))
        up = _dot(h, w3_ref[...].astype(BF16))
        act = (0.5 * (gate * _sigmoid(gate) * up)).astype(BF16)
        return _dot(act, w2_ref[...].astype(BF16))

    @pl.when(i == 0)
    def _():
        y = half_swiglu(h_ref[...])
        o_ref[...] += y[:tm]
        os_ref[...] += y[tm:tm + ns]

    @pl.when(i > 0)
    def _():
        o_ref[...] += half_swiglu(h_ref[:tm, :])


def ffn(x, xs, g, w13, w2, layer, half):
    m, d = x.shape
    ns = xs.shape[0]
    assert ns <= FFN_RIDERS
    tm = min(FFN_TM, m)
    tf = FFN_TF
    nf = D_FF // tf
    return pl.pallas_call(
        _ffn_kernel,
        grid=(m // tm, nf),
        in_specs=[
            pl.BlockSpec((tm, d), lambda i, f: (i, 0)),
            pl.BlockSpec((ns, d), lambda i, f: (0, 0)),
            pl.BlockSpec((1, d), lambda i, f: (0, 0)),
            pl.BlockSpec((None, None, d, tf), lambda i, f: (layer, half, 0, f)),
            pl.BlockSpec((None, None, d, tf), lambda i, f: (layer, half, 0, f + nf)),
            pl.BlockSpec((None, None, tf, d), lambda i, f: (layer, half, f, 0)),
        ],
        out_specs=[pl.BlockSpec((tm, d), lambda i, f: (i, 0)), pl.BlockSpec((ns, d), lambda i, f: (0, 0))],
        out_shape=[jax.ShapeDtypeStruct((m, d), F32), jax.ShapeDtypeStruct((ns, d), F32)],
        scratch_shapes=[pltpu.VMEM((tm + FFN_RIDERS, d), BF16)],
        compiler_params=_params("parallel", "arbitrary"),
        name="ffn",
    )(x, xs, g.reshape(1, d), w13, w13, w2)


def _norm_mm_kernel(*refs, qk_epilogue, mxu_copy_scale, f32_layout):
    refs = list(refs)
    ob_ref = refs.pop() if mxu_copy_scale is not None else None
    o_ref = refs.pop() if f32_layout is not None else None
    x_ref, g_ref, w_ref = refs[:3]

    def emit(j, val):
        cols = slice(j * LANES, (j + 1) * LANES)
        if f32_layout == "rows":
            o_ref[:, cols] = val
        elif f32_layout == "transposed":
            o_ref[0, cols, :] = val.T
        elif f32_layout == "heads":
            o_ref[0, j] = val
        if ob_ref is not None:
            ob_ref[:, cols] = (val * mxu_copy_scale).astype(BF16)

    y = _dot(_rms(x_ref[...], g_ref[...], NORM_EPS).astype(BF16), w_ref[...])
    if not qk_epilogue:
        for j in range(y.shape[1] // LANES):
            emit(j, y[:, j * LANES:(j + 1) * LANES])
        return
    gh_ref, cos_ref, s1_ref, s2_ref = refs[3:7]
    gm = _group_matrix(1.0 / DIFF_DH)
    gh, cos, s1, s2 = gh_ref[...], cos_ref[...], s1_ref[...], s2_ref[...]
    half = ROT_DIM // 2
    for j in range(y.shape[1] // LANES):
        blk = y[:, j * LANES:(j + 1) * LANES]
        nb = blk * lax.rsqrt(_group_sum(blk * blk, gm) + NORM_EPS) * gh
        emit(j, nb * cos + pltpu.roll(nb, LANES - half, 1) * s1 + pltpu.roll(nb, half, 1) * s2)


def norm_mm(x, g, w, tm, rows_per_seq, rope=None, head_gain=None, mxu_copy_scale=None, f32_layout="rows"):
    m, d = x.shape
    n = w.shape[1]
    qk = rope is not None
    tiles_per_seq = rows_per_seq // tm
    seqs = m // rows_per_seq
    fixed = lambda i: (0, 0)
    in_specs = [pl.BlockSpec((tm, d), lambda i: (i, 0)), pl.BlockSpec((1, d), fixed), pl.BlockSpec((d, n), fixed)]
    args = [x, g.reshape(1, d), w]
    if qk:
        in_specs.append(pl.BlockSpec((1, LANES), fixed))
        args.append(jnp.tile(head_gain.reshape(1, DIFF_DH), (1, 2)))
        for t in rope:
            in_specs.append(pl.BlockSpec((tm, LANES), lambda i: (i % tiles_per_seq, 0)))
            args.append(t)
    out_specs, out_shape = [], []
    if f32_layout == "rows":
        out_specs.append(pl.BlockSpec((tm, n), lambda i: (i, 0)))
        out_shape.append(jax.ShapeDtypeStruct((m, n), F32))
    elif f32_layout == "transposed":
        out_specs.append(pl.BlockSpec((1, n, tm), lambda i: (i // tiles_per_seq, 0, i % tiles_per_seq)))
        out_shape.append(jax.ShapeDtypeStruct((seqs, n, rows_per_seq), F32))
    elif f32_layout == "heads":
        out_specs.append(pl.BlockSpec((1, n // LANES, tm, LANES),
                                      lambda i: (i // tiles_per_seq, 0, i % tiles_per_seq, 0)))
        out_shape.append(jax.ShapeDtypeStruct((seqs, n // LANES, rows_per_seq, LANES), F32))
    if mxu_copy_scale is not None:
        out_specs.append(pl.BlockSpec((tm, n), lambda i: (i, 0)))
        out_shape.append(jax.ShapeDtypeStruct((m, n), BF16))
    return pl.pallas_call(
        functools.partial(_norm_mm_kernel, qk_epilogue=qk, mxu_copy_scale=mxu_copy_scale, f32_layout=f32_layout),
        grid=(m // tm,),
        in_specs=in_specs,
        out_specs=out_specs,
        out_shape=out_shape,
        compiler_params=_params("parallel"),
        name="norm_mm_qk" if qk else "norm_mm",
    )(*args)


def rope_tables(pos):
    half = ROT_DIM // 2
    inv = ROPE_THETA ** (-jnp.arange(0, ROT_DIM, 2, dtype=F32) / ROT_DIM)
    ang = pos.astype(F32)[:, None] * inv[None, :]
    cos, sin = jnp.cos(ang), jnp.sin(ang)
    t = pos.shape[0]
    rest = DIFF_DH - ROT_DIM
    c64 = jnp.concatenate([cos, cos, jnp.ones((t, rest), F32)], axis=1)
    s1_64 = jnp.concatenate([-sin, jnp.zeros((t, DIFF_DH - half), F32)], axis=1)
    s2_64 = jnp.concatenate([jnp.zeros((t, half), F32), sin, jnp.zeros((t, rest), F32)], axis=1)
    return tuple(jnp.tile(z, (1, 2)) for z in (c64, s1_64, s2_64))


def _mem_kv_kernel(x_ref, g_ref, w_ref, gk_ref, k_ref, v_ref):
    h = _rms(x_ref[...], g_ref[...], NORM_EPS).astype(BF16)
    y = _dot(h, w_ref[...])
    gk = gk_ref[...]
    for j in range(MEM_HEADS):
        blk = y[:, j * MEM_DH:(j + 1) * MEM_DH]
        k_ref[:, j * MEM_DH:(j + 1) * MEM_DH] = _rms(blk, gk, NORM_EPS)
    v_ref[...] = y[:, MEM_W:]


def mem_kv(mem, g, w, gk):
    m, d = mem.shape
    return pl.pallas_call(
        _mem_kv_kernel,
        grid=(1,),
        in_specs=[
            pl.BlockSpec((m, d), lambda i: (0, 0)),
            pl.BlockSpec((1, d), lambda i: (0, 0)),
            pl.BlockSpec((d, 2 * MEM_W), lambda i: (0, 0)),
            pl.BlockSpec((1, MEM_DH), lambda i: (0, 0)),
        ],
        out_specs=[pl.BlockSpec((m, MEM_W), lambda i: (0, 0))] * 2,
        out_shape=[jax.ShapeDtypeStruct((m, MEM_W), F32)] * 2,
        compiler_params=_params("arbitrary"),
        name="mem_kv",
    )(mem, g.reshape(1, d), w, gk.reshape(1, MEM_DH))


def _mem_attn_kernel(q_ref, k_ref, v_ref, gq_ref, o_ref):
    q = q_ref[0]
    rows = q.shape[0]
    if rows < 8:
        q = jnp.broadcast_to(q, (8, q.shape[1]))
    k = k_ref[0]
    v = v_ref[0]
    gq = gq_ref[...]
    for h in range(MEM_HEADS):
        sl = slice(h * MEM_DH, (h + 1) * MEM_DH)
        qh = _rms(q[:, sl], gq, NORM_EPS).astype(BF16)
        s = _dot(qh, k[:, sl].astype(BF16), NT_DIMS) * (MEM_DH ** -0.5)
        p = jnp.exp(s - jnp.max(s, axis=-1, keepdims=True))
        o = _dot(p.astype(BF16), v[:, sl].astype(BF16)) / jnp.sum(p, axis=-1, keepdims=True)
        o_ref[0, :, sl] = o[:rows].astype(o_ref.dtype)


def mem_attn(proj, q_col_block, mk, mv, gq, tq):
    b, t, _ = proj.shape
    return pl.pallas_call(
        _mem_attn_kernel,
        grid=(b, t // tq),
        in_specs=[
            pl.BlockSpec((1, tq, MEM_W), lambda i, j: (i, j, q_col_block)),
            pl.BlockSpec((1, MEM_TOKENS, MEM_W), lambda i, j: (i, 0, 0)),
            pl.BlockSpec((1, MEM_TOKENS, MEM_W), lambda i, j: (i, 0, 0)),
            pl.BlockSpec((1, MEM_DH), lambda i, j: (0, 0)),
        ],
        out_specs=pl.BlockSpec((1, tq, MEM_W), lambda i, j: (i, j, 0)),
        out_shape=jax.ShapeDtypeStruct((b, t, MEM_W), BF16),
        compiler_params=_params("parallel", "arbitrary"),
        name="mem_attn",
    )(proj, mk, mv, gq.reshape(1, MEM_DH))


def _out_mm_kernel(x_ref, a_ref, b_ref, wa_ref, wb_ref, o_ref):
    o_ref[...] = x_ref[...] + _dot(a_ref[...], wa_ref[...]) + _dot(b_ref[...], wb_ref[...])


def out_mm(x, a, b, w, layer, tm):
    m, d = x.shape
    tn = d
    ka, kb = a.shape[1], b.shape[1]
    kb_blocks = ka // kb
    return pl.pallas_call(
        _out_mm_kernel,
        grid=(m // tm, d // tn),
        in_specs=[
            pl.BlockSpec((tm, tn), lambda i, j: (i, j)),
            pl.BlockSpec((tm, ka), lambda i, j: (i, 0)),
            pl.BlockSpec((tm, kb), lambda i, j: (i, 0)),
            pl.BlockSpec((None, ka, tn), lambda i, j: (layer, 0, j)),
            pl.BlockSpec((None, kb, tn), lambda i, j: (layer, kb_blocks, j)),
        ],
        out_specs=pl.BlockSpec((tm, tn), lambda i, j: (i, j)),
        out_shape=jax.ShapeDtypeStruct((m, d), F32),
        compiler_params=_params("parallel", "arbitrary"),
        name="out_mm",
    )(x, a, b, w, w)


def _a_prep_kernel(x_ref, xb_ref, sp_ref, g_ref, mu_ref, xl_ref, cat_ref, mix_ref, *, tiles_per_seq):
    g = g_ref[...]
    xn = _rms(x_ref[...], g, NORM_EPS)
    if tiles_per_seq is None:
        xp = sp_ref[...]
    else:
        before = _rms(xb_ref[...], g, NORM_EPS)[SUBLANES - 1:SUBLANES, :]
        first = pl.program_id(0) % tiles_per_seq == 0
        before = jnp.where(first, sp_ref[0], before)
        row = lax.broadcasted_iota(jnp.int32, xn.shape, 0)
        xp = jnp.where(row == 0, before, pltpu.roll(xn, 1, 0))
    xx = xp - xn
    d = xn.shape[1]
    xl_ref[0] = xn[xn.shape[0] - SUBLANES:, :]
    cat_ref[:, :d] = xn.astype(BF16)
    cat_ref[:, d:] = xx.astype(BF16)
    for i in range(3):
        mix_ref[:, i * d:(i + 1) * d] = (xn + xx * mu_ref[i:i + 1, :]).astype(BF16)


def a_prep(x, shift_prev, g, mu, tm, rows_per_seq):
    m, d = x.shape
    if rows_per_seq == 1:
        assert m == tm == SUBLANES
        tiles_per_seq, groups = None, 1
        sp, sp_spec = shift_prev, pl.BlockSpec((tm, d), lambda i: (i, 0))
        last_map = lambda i: (0, 0, 0)
        before_map = lambda i: (0, 0)
    else:
        tiles_per_seq, groups = rows_per_seq // tm, m // rows_per_seq
        sp = shift_prev.reshape(shift_prev.shape[0], 1, d)
        sp_spec = pl.BlockSpec((1, 1, d), lambda i: (i // tiles_per_seq, 0, 0))
        last_map = lambda i: (i // tiles_per_seq, 0, 0)
        before_map = lambda i: (jnp.maximum(i * (tm // SUBLANES) - 1, 0), 0)
    return pl.pallas_call(
        functools.partial(_a_prep_kernel, tiles_per_seq=tiles_per_seq),
        grid=(m // tm,),
        in_specs=[
            pl.BlockSpec((tm, d), lambda i: (i, 0)),
            pl.BlockSpec((SUBLANES, d), before_map),
            sp_spec,
            pl.BlockSpec((1, d), lambda i: (0, 0)),
            pl.BlockSpec((3, d), lambda i: (0, 0)),
        ],
        out_specs=[
            pl.BlockSpec((1, SUBLANES, d), last_map),
            pl.BlockSpec((tm, 2 * d), lambda i: (i, 0)),
            pl.BlockSpec((tm, 3 * d), lambda i: (i, 0)),
        ],
        out_shape=[
            jax.ShapeDtypeStruct((groups, SUBLANES, d), F32),
            jax.ShapeDtypeStruct((m, 2 * d), BF16),
            jax.ShapeDtypeStruct((m, 3 * d), BF16),
        ],
        compiler_params=_params("arbitrary"),
        name="a_prep",
    )(x, x, sp, g.reshape(1, d), mu)


def _mm_kernel(a_ref, w_ref, o_ref):
    o_ref[...] = _dot(a_ref[...], w_ref[...])


def mm(a, w, layer, tm, tn=1024):
    m, k = a.shape
    n = w.shape[2]
    return pl.pallas_call(
        _mm_kernel,
        grid=(m // tm, n // tn),
        in_specs=[pl.BlockSpec((tm, k), lambda i, j: (i, 0)),
                  pl.BlockSpec((None, k, tn), lambda i, j: (layer, 0, j))],
        out_specs=pl.BlockSpec((tm, tn), lambda i, j: (i, j)),
        out_shape=jax.ShapeDtypeStruct((m, n), F32),
        compiler_params=_params("parallel", "arbitrary"),
        name="mm",
    )(a, w)


def _a_mix_kernel(k_ref, mix_ref, w1_ref, w2_ref, a1_ref, a2_ref, g1_ref, g2_ref, w0_ref, a0_ref, kk_ref,
                  ka_ref, ld_ref, kp_ref, kn_ref, b_ref, g_ref):
    d = D_MODEL
    xw, xa, xg = mix_ref[:, :d], mix_ref[:, d:2 * d], mix_ref[:, 2 * d:]
    wl = _dot(jnp.tanh(_dot(xw, w1_ref[...])).astype(BF16), w2_ref[...]) + w0_ref[...]
    ld_ref[...] = -math.exp(-0.5) * _sigmoid(wl)
    a = _sigmoid(_dot(_dot(xa, a1_ref[...]).astype(BF16), a2_ref[...]) + a0_ref[...])
    g_ref[...] = _dot(_sigmoid(_dot(xg, g1_ref[...])).astype(BF16), g2_ref[...])
    k = k_ref[...]
    kp_ref[...] = k * (1.0 + (a - 1.0) * ka_ref[...])
    kraw = k * kk_ref[...]
    gm = _group_matrix(1.0)
    for j in range(RWKV_W // LANES):
        sl = slice(j * LANES, (j + 1) * LANES)
        blk = kraw[:, sl]
        kn = blk / jnp.maximum(jnp.sqrt(_group_sum(blk * blk, gm)), 1e-12)
        kn_ref[:, sl] = kn
        b_ref[:, sl] = kn * a[:, sl]


def a_mix(proj, mix, lw, tm):
    m = proj.shape[0]
    row = lambda i: (i, 0)
    fixed = lambda i: (0, 0)
    vec = pl.BlockSpec((1, RWKV_W), fixed)
    out = jax.ShapeDtypeStruct((m, RWKV_W), F32)
    return pl.pallas_call(
        _a_mix_kernel,
        grid=(m // tm,),
        in_specs=[
            pl.BlockSpec((tm, RWKV_W), lambda i: (i, 1)),
            pl.BlockSpec((tm, 3 * D_MODEL), row),
            pl.BlockSpec((D_MODEL, LORA_PAD), fixed), pl.BlockSpec((LORA_PAD, RWKV_W), fixed),
            pl.BlockSpec((D_MODEL, LORA_PAD), fixed), pl.BlockSpec((LORA_PAD, RWKV_W), fixed),
            pl.BlockSpec((D_MODEL, LORA_G), fixed), pl.BlockSpec((LORA_G, RWKV_W), fixed),
            vec, vec, vec, vec,
        ],
        out_specs=[pl.BlockSpec((tm, RWKV_W), row)] * 5,
        out_shape=[out] * 5,
        compiler_params=_params("parallel"),
        name="a_mix",
    )(proj, mix, lw["w1"], lw["w2"], lw["a1"], lw["a2"], lw["g1"], lw["g2"],
      lw["w0"], lw["a0"], lw["k_k"], lw["k_a"])


def _wkv_masks(c):
    c2 = 2 * c
    ri = lax.broadcasted_iota(jnp.int32, (c2, c2), 0)
    ci = lax.broadcasted_iota(jnp.int32, (c2, c2), 1)
    same = (ri // c) == (ci // c)
    masks = dict(
        in_head0=lax.broadcasted_iota(jnp.int32, (c, LANES), 1) < RWKV_N,
        tri=jnp.where(lax.broadcasted_iota(jnp.int32, (c, c), 0) >= lax.broadcasted_iota(jnp.int32, (c, c), 1),
                      1.0, 0.0).astype(BF16),
        strict=jnp.logical_and(same, (ri % c) > (ci % c)),
        incl=jnp.logical_and(same, (ri % c) >= (ci % c)),
        eye=jnp.where(ri == ci, 1.0, 0.0),
        diag8=(ri // 8) == (ci // 8),
        lower_left=[],
    )
    size = 8
    while size < c:
        masks["lower_left"].append(jnp.logical_and(
            (ri // (2 * size)) == (ci // (2 * size)),
            jnp.logical_and((ri // size) % 2 == 1, (ci // size) % 2 == 0)))
        size *= 2
    return masks


def _each(f, *lists):
    return [f(*args) for args in zip(*lists)]


def _wkv_chunk(s, r, ld, k, v, kn, b, mk):
    c = r[0].shape[0]
    c2 = 2 * c
    in_head0 = mk["in_head0"]
    tri = mk["tri"]

    def stack(z):
        return jnp.concatenate([jnp.where(in_head0, z, 0.0), jnp.where(in_head0, 0.0, z)], axis=0)

    def bdot(x, y):
        return _dot(x.astype(BF16), y.astype(BF16))

    p1 = _each(lambda z: z.astype(BF16), ld)
    rem = _each(lambda z, p: z - p.astype(F32), ld, p1)
    p2 = _each(lambda z: z.astype(BF16), rem)
    p3 = _each(lambda z, p: (z - p.astype(F32)).astype(BF16), rem, p2)
    cum = _each(lambda a1, a2, a3: _dot(tri, a1) + _dot(tri, a2) + _dot(tri, a3), p1, p2, p3)
    eg = _each(jnp.exp, cum)
    einv = _each(lambda z: jnp.exp(-z), cum)
    at_s = _each(lambda n, z, d: stack(-n * jnp.exp(z - d)).astype(BF16), kn, cum, ld)
    rt_s = _each(lambda x, e: stack(x * e).astype(BF16), r, eg)
    bt = _each(lambda x, e: x * e, b, einv)
    kt = _each(lambda x, e: x * e, k, einv)
    v_s = _each(lambda x: stack(x).astype(BF16), v)

    gmat = _each(lambda a, x, y, z: _dot(jnp.concatenate([a, x], axis=0),
                                         jnp.concatenate([y, y, z, z], axis=0).astype(BF16), NT_DIMS),
                 at_s, rt_s, bt, kt)
    a_ab = _each(lambda g: jnp.where(mk["strict"], g[:c2, :c2], 0.0), gmat)
    a_ak = _each(lambda g: jnp.where(mk["strict"], g[:c2, c2:], 0.0).astype(BF16), gmat)
    a_rr = _each(lambda g: jnp.concatenate([jnp.where(mk["incl"], g[c2:, :c2], 0.0),
                                            jnp.where(mk["incl"], g[c2:, c2:], 0.0)], axis=1).astype(BF16), gmat)

    ad = _each(lambda a: jnp.where(mk["diag8"], a, 0.0), a_ab)
    ad2 = _each(bdot, ad, ad)
    ad4 = _each(bdot, ad2, ad2)
    inv = _each(lambda a: mk["eye"] + a, ad)
    inv = _each(lambda x, y: x + bdot(x, y), inv, ad2)
    inv = _each(lambda x, y: x + bdot(x, y), inv, ad4)
    for lower_left in mk["lower_left"]:
        half = _each(lambda x, a: bdot(x, jnp.where(lower_left, a, 0.0)), inv, a_ab)
        inv = _each(lambda x, y: x + bdot(y, x), inv, half)

    s_b = _each(lambda z: z.astype(BF16), s)
    rhs = _each(lambda a, x, y, z: _dot(a, x) + _dot(y, z, NT_DIMS), a_ak, v_s, at_s, s_b)
    u = _each(lambda x, y: bdot(x, y).astype(BF16), inv, rhs)
    uv = _each(lambda x, y: jnp.concatenate([x, y], axis=0), u, v_s)
    y_s = _each(lambda x, z, a, w: _dot(x, z, NT_DIMS) + _dot(a, w), rt_s, s_b, a_rr, uv)
    ds = _each(lambda w, x, y: _dot(w, jnp.concatenate([stack(x), stack(y)], axis=0).astype(BF16), TN_DIMS),
               uv, bt, kt)
    y = _each(lambda z: z[:c] + z[c:], y_s)
    s_new = _each(lambda z, dz, e: (z + dz) * e[c - 1:c, :], s, ds, eg)
    return y, s_new


def _wkv_kernel(r_ref, ld_ref, k_ref, v_ref, kn_ref, b_ref, y_ref, s_ref, st_ref, *, chunks, pairs):
    @pl.when(pl.program_id(2) == 0)
    def _():
        st_ref[...] = jnp.zeros_like(st_ref)

    mk = _wkv_masks(CHUNK)
    s = [st_ref[p] for p in range(pairs)]
    for i in range(chunks):
        rows = slice(i * CHUNK, (i + 1) * CHUNK)
        cut = lambda ref: [ref[0, rows, p * LANES:(p + 1) * LANES] for p in range(pairs)]
        y, s = _wkv_chunk(s, cut(r_ref), cut(ld_ref), cut(k_ref), cut(v_ref), cut(kn_ref), cut(b_ref), mk)
        for p in range(pairs):
            y_ref[0, rows, p * LANES:(p + 1) * LANES] = y[p]
    for p in range(pairs):
        st_ref[p] = s[p]

    @pl.when(pl.program_id(2) == pl.num_programs(2) - 1)
    def _():
        s_ref[0] = st_ref[...]


WKV_PAIRS = 12
WKV_CHUNKS = 2


def wkv_scan(proj, ld, kp, kn, b):
    bsz, t, _ = proj.shape
    groups = RWKV_W // LANES // WKV_PAIRS
    tb = WKV_CHUNKS * CHUNK
    width = WKV_PAIRS * LANES
    blk = lambda off: pl.BlockSpec((1, tb, width), lambda i, p, c: (i, c, p + off))
    return pl.pallas_call(
        functools.partial(_wkv_kernel, chunks=WKV_CHUNKS, pairs=WKV_PAIRS),
        grid=(bsz, groups, t // tb),
        in_specs=[blk(0), blk(0), blk(0), blk(2 * groups), blk(0), blk(0)],
        out_specs=[
            pl.BlockSpec((1, tb, width), lambda i, p, c: (i, c, p)),
            pl.BlockSpec((1, WKV_PAIRS, LANES, LANES), lambda i, p, c: (i, p, 0, 0)),
        ],
        out_shape=[
            jax.ShapeDtypeStruct((bsz, t, RWKV_W), F32),
            jax.ShapeDtypeStruct((bsz, RWKV_W // LANES, LANES, LANES), F32),
        ],
        scratch_shapes=[pltpu.VMEM((WKV_PAIRS, LANES, LANES), F32)],
        compiler_params=_params("parallel", "parallel", "arbitrary"),
        name="wkv_scan",
    )(proj, ld, kp, proj, kn, b)


def _wkv_step_kernel(s_ref, r_ref, ld_ref, k_ref, v_ref, kn_ref, b_ref, y_ref, so_ref):
    s = s_ref[0]
    sa = jnp.sum(s * (-kn_ref[0]), axis=-1, keepdims=True)
    s = s * jnp.exp(ld_ref[0]) + sa * b_ref[0] + v_ref[0] * k_ref[0]
    so_ref[0] = s
    y_ref[0] = jnp.sum(s * r_ref[0], axis=-1, keepdims=True)


def wkv_step(s0, r, ld, k, v, kn, b):
    bsz = s0.shape[0]
    h, n = RWKV_HEADS, RWKV_N
    st = pl.BlockSpec((1, h, n, n), lambda i: (i, 0, 0, 0))
    rw = pl.BlockSpec((1, h, 1, n), lambda i: (i, 0, 0, 0))
    cl = pl.BlockSpec((1, h, n, 1), lambda i: (i, 0, 0, 0))
    return pl.pallas_call(
        _wkv_step_kernel,
        grid=(bsz,),
        in_specs=[st, rw, rw, rw, cl, rw, rw],
        out_specs=[cl, st],
        out_shape=[jax.ShapeDtypeStruct((bsz, h, n, 1), F32), jax.ShapeDtypeStruct((bsz, h, n, n), F32)],
        compiler_params=_params("parallel"),
        name="wkv_step",
    )(s0, r, ld, k, v, kn, b)


def _a_post_kernel(y_ref, r_ref, kp_ref, v_ref, g_ref, lw_ref, lb_ref, rk_ref, o_ref):
    gsum = _group_matrix(1.0)
    gmean = _group_matrix(1.0 / RWKV_N)
    for j in range(RWKV_W // LANES):
        sl = slice(j * LANES, (j + 1) * LANES)
        y = y_ref[:, sl]
        cen = y - _group_sum(y, gmean)
        yn = cen * lax.rsqrt(_group_sum(cen * cen, gmean) + LNX_EPS) * lw_ref[:, sl] + lb_ref[:, sl]
        bonus = _group_sum(r_ref[:, sl] * kp_ref[:, sl] * rk_ref[:, sl], gsum) * v_ref[:, sl]
        o_ref[:, sl] = ((yn + bonus) * g_ref[:, sl]).astype(o_ref.dtype)


def a_post(y, proj, kp, g, lnx_w, lnx_b, r_k, tm):
    m = y.shape[0]
    row = lambda i: (i, 0)
    vec = pl.BlockSpec((1, RWKV_W), lambda i: (0, 0))
    tile = pl.BlockSpec((tm, RWKV_W), row)
    return pl.pallas_call(
        _a_post_kernel,
        grid=(m // tm,),
        in_specs=[tile, pl.BlockSpec((tm, RWKV_W), lambda i: (i, 0)), tile,
                  pl.BlockSpec((tm, RWKV_W), lambda i: (i, 2)), tile, vec, vec, vec],
        out_specs=tile,
        out_shape=jax.ShapeDtypeStruct((m, RWKV_W), BF16),
        compiler_params=_params("parallel"),
        name="a_post",
    )(y, proj, kp, proj, g, lnx_w.reshape(1, RWKV_W), lnx_b.reshape(1, RWKV_W), r_k.reshape(1, RWKV_W))


def _lambda(lam_ref, lam_init):
    lq = lam_ref[...]
    l1 = jnp.sum(lq[0:1] * lq[1:2], axis=-1, keepdims=True)
    l2 = jnp.sum(lq[2:3] * lq[3:4], axis=-1, keepdims=True)
    return jnp.exp(l1) - jnp.exp(l2) + lam_init


ATTN_HEADS = 2
ATTN_TQ = 512
LOG2E = 1.4426950408889634
Q_SCALE = DIFF_DH ** -0.5 * LOG2E


def _diff_attn_kernel(qi_ref, ki_ref, q_ref, k_ref, v_ref, lam_ref, gs_ref, o_ref, m_ref, l_ref, acc_ref, *,
                      tq, lam_init):
    qi = qi_ref[pl.program_id(2)]
    ki = ki_ref[pl.program_id(2)]

    @pl.when(ki == 0)
    def _():
        m_ref[...] = jnp.full_like(m_ref, -jnp.inf)
        l_ref[...] = jnp.zeros_like(l_ref)
        acc_ref[...] = jnp.zeros_like(acc_ref)

    def step(diagonal):
        lane = lax.broadcasted_iota(jnp.int32, (tq, DIFF_DV), 1)
        if diagonal:
            visible = (lax.broadcasted_iota(jnp.int32, (tq, tq), 1)
                       <= lax.broadcasted_iota(jnp.int32, (tq, tq), 0))
        for h in range(ATTN_HEADS):
            cols = slice(h * DIFF_DV, (h + 1) * DIFF_DV)
            q = q_ref[0, :, cols]
            kb = k_ref[0, :, cols]
            v1 = jnp.concatenate([v_ref[0, :, cols], jnp.ones((tq, LANES), BF16)], axis=1)
            for c in range(2):
                idx = 2 * h + c
                in_comp = (lane < DIFF_DH) if c == 0 else (lane >= DIFF_DH)
                s = _dot(jnp.where(in_comp, q, jnp.zeros_like(q)), kb, NT_DIMS)
                if diagonal:
                    s = jnp.where(visible, s, -jnp.inf)
                m_old = m_ref[idx]
                m_new = jnp.maximum(m_old, jnp.max(s, axis=-1, keepdims=True))
                alpha = jnp.exp2(m_old - m_new)
                p = jnp.exp2(s - jnp.concatenate([m_new] * (tq // LANES), axis=1))
                pv = _dot(p.astype(BF16), v1)
                l_ref[idx] = alpha * l_ref[idx] + pv[:, DIFF_DV:]
                acc_ref[idx] = alpha * acc_ref[idx] + pv[:, :DIFF_DV]
                m_ref[idx] = m_new

    @pl.when(ki < qi)
    def _():
        step(False)

    @pl.when(ki == qi)
    def _():
        step(True)
        lam = _lambda(lam_ref, lam_init)
        for h in range(ATTN_HEADS):
            o = acc_ref[2 * h] / l_ref[2 * h] - lam * (acc_ref[2 * h + 1] / l_ref[2 * h + 1])
            o_ref[0, :, h * DIFF_DV:(h + 1) * DIFF_DV] = (
                _rms(o, gs_ref[...], SUBLN_EPS) * (1.0 - lam_init)).astype(o_ref.dtype)


def diff_attn_prompt(q, k, v, lam_p, subln, lam_init):
    b, t, _ = q.shape
    tq = ATTN_TQ
    width = ATTN_HEADS * DIFF_DV
    chains = 2 * ATTN_HEADS
    blocks = t // tq
    pairs = [(qi, ki) for qi in range(blocks) for ki in range(qi + 1)]
    q_of = jnp.asarray([p[0] for p in pairs], jnp.int32)
    k_of = jnp.asarray([p[1] for p in pairs], jnp.int32)
    q_map = lambda i, h, p, q_of, k_of: (i, q_of[p], h)
    kv_map = lambda i, h, p, q_of, k_of: (i, k_of[p], h)
    fixed = lambda i, h, p, q_of, k_of: (0, 0)
    grid_spec = pltpu.PrefetchScalarGridSpec(
        num_scalar_prefetch=2,
        grid=(b, DIFF_HEADS // ATTN_HEADS, len(pairs)),
        in_specs=[
            pl.BlockSpec((1, tq, width), q_map),
            pl.BlockSpec((1, tq, width), kv_map),
            pl.BlockSpec((1, tq, width), kv_map),
            pl.BlockSpec((4, DIFF_DH), fixed),
            pl.BlockSpec((1, DIFF_DV), fixed),
        ],
        out_specs=pl.BlockSpec((1, tq, width), q_map),
        scratch_shapes=[pltpu.VMEM((chains, tq, LANES), F32), pltpu.VMEM((chains, tq, LANES), F32),
                        pltpu.VMEM((chains, tq, DIFF_DV), F32)],
    )
    return pl.pallas_call(
        functools.partial(_diff_attn_kernel, tq=tq, lam_init=lam_init),
        grid_spec=grid_spec,
        out_shape=jax.ShapeDtypeStruct((b, t, DIFF_W), BF16),
        compiler_params=_params("parallel", "parallel", "arbitrary"),
        name="diff_attn_prompt",
    )(q_of, k_of, q, k, v, lam_p, subln.reshape(1, DIFF_DV))


DEC_PAGES = 4
DEC_ROWS = 8


def _dec_attn_kernel(pt_ref, q_ref, kn_ref, vn_ref, lam_ref, gs_ref, *refs, lam_init):
    k_refs = refs[:DEC_PAGES]
    v_refs = refs[DEC_PAGES:2 * DEC_PAGES]
    o_ref, qb_ref, m_ref, l_ref, acc_ref = refs[2 * DEC_PAGES:]
    step = pl.program_id(1)
    row = lax.broadcasted_iota(jnp.int32, (DIFF_HEADS, DEC_ROWS, LANES), 1)

    def comp_rows(prod):
        s0 = jnp.sum(prod[:, :DIFF_DH, :], axis=1, keepdims=True)
        s1 = jnp.sum(prod[:, DIFF_DH:, :], axis=1, keepdims=True)
        return jnp.where(row == 0, s0, jnp.where(row == 1, s1, 0.0))

    @pl.when(step == 0)
    def _():
        q_col = q_ref[0] * (DIFF_DH ** -0.5)
        qb_ref[...] = jnp.broadcast_to(q_col, qb_ref.shape)
        m_ref[...] = comp_rows(q_col * kn_ref[0])
        l_ref[...] = jnp.ones_like(l_ref)
        acc_ref[...] = jnp.broadcast_to(vn_ref[0], acc_ref.shape)

    qb = qb_ref[...]
    for k_ref, v_ref in zip(k_refs, v_refs):
        s = comp_rows(k_ref[0] * qb)
        m_old = m_ref[...]
        m_new = jnp.maximum(m_old, jnp.max(s, axis=-1, keepdims=True))
        alpha = jnp.exp(m_old - m_new)
        p = jnp.exp(s - m_new)
        l_ref[...] = alpha * l_ref[...] + jnp.sum(p, axis=-1, keepdims=True)
        pv = lax.dot_general(p.astype(BF16), v_ref[0].astype(BF16), (((2,), (1,)), ((0,), (0,))),
                             preferred_element_type=F32)
        acc_ref[...] = alpha * acc_ref[...] + pv
        m_ref[...] = m_new

    @pl.when(step == pl.num_programs(1) - 1)
    def _():
        lam = _lambda(lam_ref, lam_init)
        w = acc_ref[...] / l_ref[...]
        o = w[:, 0:1, :] - lam * w[:, 1:2, :]
        o = o * lax.rsqrt(jnp.mean(o * o, axis=-1, keepdims=True) + SUBLN_EPS) * gs_ref[...] * (1.0 - lam_init)
        o_ref[0] = jnp.broadcast_to(o, o_ref.shape[1:])


def diff_attn_decode(q_col, k_col, v_new, cache_kt, cache_v, page_table, lam_p, subln, lam_init):
    b = q_col.shape[0]
    n_pages = page_table.shape[1]
    col = pl.BlockSpec((1, DIFF_HEADS, DIFF_DV, 1), lambda i, s, pt: (i, 0, 0, 0))
    state = pltpu.VMEM((DIFF_HEADS, DEC_ROWS, LANES), F32)

    def page_spec(j):
        return pl.BlockSpec((1, DIFF_HEADS, PAGE_SIZE, DIFF_DV), lambda i, s, pt: (pt[i, s * DEC_PAGES + j], 0, 0, 0))

    grid_spec = pltpu.PrefetchScalarGridSpec(
        num_scalar_prefetch=1,
        grid=(b, n_pages // DEC_PAGES),
        in_specs=[col, col,
                  pl.BlockSpec((1, DIFF_HEADS, 1, DIFF_DV), lambda i, s, pt: (i, 0, 0, 0)),
                  pl.BlockSpec((4, DIFF_DH), lambda i, s, pt: (0, 0)),
                  pl.BlockSpec((1, DIFF_DV), lambda i, s, pt: (0, 0))]
        + [page_spec(j) for j in range(DEC_PAGES)] * 2,
        out_specs=pl.BlockSpec((1, DIFF_HEADS, DEC_ROWS, DIFF_DV), lambda i, s, pt: (i, 0, 0, 0)),
        scratch_shapes=[pltpu.VMEM((DIFF_HEADS, DIFF_DV, PAGE_SIZE), F32), state, state, state],
    )
    return pl.pallas_call(
        functools.partial(_dec_attn_kernel, lam_init=lam_init),
        grid_spec=grid_spec,
        out_shape=jax.ShapeDtypeStruct((b, DIFF_HEADS, DEC_ROWS, DIFF_DV), F32),
        compiler_params=_params("parallel", "arbitrary"),
        name="diff_attn_decode",
    )(page_table, q_col, k_col, v_new, lam_p, subln.reshape(1, DIFF_DV),
      *([cache_kt] * DEC_PAGES), *([cache_v] * DEC_PAGES))


def _pad_lora(w_in, w_out):
    pad = LORA_PAD - w_in.shape[1]
    return (jnp.pad(w_in, ((0, 0), (0, pad))).astype(BF16), jnp.pad(w_out, ((0, pad), (0, 0))).astype(BF16))


def kernel(x_prompt, x_sample, mem_prompt, state_wkv, state_shift, cache_mem_k, cache_mem_v, cache_k, cache_v, page_table, ffn_norm, ffn_w13, ffn_w2, mix_norm, w_out, mem_norm, mem_w_kv, mem_q_norm, mem_k_norm, a_w_in, a_mu, a_w0, a_w1, a_w2, a_a0, a_a1, a_a2, a_g1, a_g2, a_k_k, a_k_a, a_r_k, a_lnx_w, a_lnx_b, kv_norm, kv_w, k_norm, b_w_in, b_q_norm, b_lam, b_subln):
    d = D_MODEL
    wout_b = w_out.astype(BF16)
    memw_b = mem_w_kv.astype(BF16)
    awin_b = a_w_in.astype(BF16)
    kvw_k, kvw_v = kv_w[:, :DIFF_W].astype(BF16), kv_w[:, DIFF_W:].astype(BF16)
    bq_b, bm_b = b_w_in[:, :, :DIFF_W].astype(BF16), b_w_in[:, :, DIFF_W:].astype(BF16)
    loras = []
    for i in range(N_A):
        w1, w2 = _pad_lora(a_w1[i], a_w2[i])
        a1, a2 = _pad_lora(a_a1[i], a_a2[i])
        loras.append(dict(w1=w1, w2=w2, a1=a1, a2=a2, g1=a_g1[i].astype(BF16), g2=a_g2[i].astype(BF16),
                          w0=a_w0[i].reshape(1, RWKV_W), a0=a_a0[i].reshape(1, RWKV_W),
                          k_k=a_k_k[i].reshape(1, RWKV_W), k_a=a_k_a[i].reshape(1, RWKV_W)))

    def run(x3, shift_prev, wkv0, mk, mv, pos, decode):
        bsz, t, _ = x3.shape
        m = bsz * t
        tm = min(512, m)
        x = x3.reshape(m, d)
        rope = rope_tables(jnp.broadcast_to(pos, (m,)) if decode else pos)
        rope_rows = m if decode else t
        shifts, states = [], []
        for i in range(N_A):
            x = yield x, i, 0
            x_last, cat, mix = a_prep(x, shift_prev[i], mix_norm[i], a_mu[i], tm, t)
            shifts.append(x_last[0] if decode else x_last[:, SUBLANES - 1, :])
            proj = mm(cat, awin_b, i, tm)
            ld, kp, kn, bvec, gate = a_mix(proj, mix, loras[i], min(256, m))
            proj3 = proj.reshape(bsz, t, A_IN)
            if decode:
                heads = lambda z: z.reshape(bsz, RWKV_HEADS, 1, RWKV_N)
                y, s_new = wkv_step(
                    wkv0[i], heads(proj[:, :RWKV_W]), heads(ld), heads(kp),
                    proj[:, 2 * RWKV_W:3 * RWKV_W].reshape(bsz, RWKV_HEADS, RWKV_N, 1), heads(kn), heads(bvec))
                y = y.reshape(m, RWKV_W)
            else:
                r3 = lambda z: z.reshape(bsz, t, RWKV_W)
                y, s_pairs = wkv_scan(proj3, r3(ld), r3(kp), r3(kn), r3(bvec))
                y = y.reshape(m, RWKV_W)
                s_new = jnp.stack([s_pairs[:, :, :RWKV_N, :RWKV_N], s_pairs[:, :, RWKV_N:, RWKV_N:]], axis=2)
                s_new = s_new.reshape(bsz, RWKV_HEADS, RWKV_N, RWKV_N)
            states.append(s_new)
            y_mix = a_post(y, proj, kp, gate, a_lnx_w[i], a_lnx_b[i], a_r_k[i], min(256, m))
            o_mem = mem_attn(proj3, 3 * RWKV_W // MEM_W, mk[i], mv[i], mem_q_norm[i], min(512, t))
            x = out_mm(x, y_mix, o_mem.reshape(m, MEM_W), wout_b, i, tm)
            x = yield x, i, 1
        if decode:
            k_rows, = norm_mm(x, kv_norm, kvw_k, tm, rope_rows, rope, k_norm)
            v_rows, = norm_mm(x, kv_norm, kvw_v, tm, rope_rows)
            k_col = k_rows.reshape(bsz, DIFF_HEADS, DIFF_DV, 1)
            v_row = v_rows.reshape(bsz, DIFF_HEADS, 1, DIFF_DV)
            k_out = k_rows.reshape(bsz, t, DIFF_HEADS, 2, DIFF_DH)
            v_out = v_rows.reshape(bsz, t, DIFF_HEADS, DIFF_DV)
        else:
            k_t, k_b = norm_mm(x, kv_norm, kvw_k, tm, rope_rows, rope, k_norm, mxu_copy_scale=1.0,
                               f32_layout="transposed")
            v_h, v_b = norm_mm(x, kv_norm, kvw_v, tm, rope_rows, mxu_copy_scale=1.0, f32_layout="heads")
            k_b, v_b = k_b.reshape(bsz, t, DIFF_W), v_b.reshape(bsz, t, DIFF_W)
            k_out = jnp.transpose(k_t.reshape(bsz, DIFF_HEADS, 2, DIFF_DH, t), (0, 4, 1, 2, 3))
            v_out = jnp.transpose(v_h, (0, 2, 1, 3))
        for j in range(N_B):
            i = N_A + j
            lam_init = 0.8 - 0.6 * math.exp(-0.3 * i)
            x = yield x, i, 0
            q_mem = norm_mm(x, mix_norm[i], bm_b[j], tm, rope_rows)[0].reshape(bsz, t, MEM_W)
            if decode:
                q, = norm_mm(x, mix_norm[i], bq_b[j], tm, rope_rows, rope, b_q_norm[j])
                o = diff_attn_decode(q.reshape(bsz, DIFF_HEADS, DIFF_DV, 1), k_col, v_row, cache_kt, cache_vt,
                                     page_table, b_lam[j], b_subln[j], lam_init)
                o = o[:, :, 0, :].astype(BF16)
            else:
                q_b, = norm_mm(x, mix_norm[i], bq_b[j], tm, rope_rows, rope, b_q_norm[j], mxu_copy_scale=Q_SCALE,
                               f32_layout=None)
                o = diff_attn_prompt(q_b.reshape(bsz, t, DIFF_W), k_b, v_b, b_lam[j], b_subln[j], lam_init)
            o_mem = mem_attn(q_mem, 0, mk[i], mv[i], mem_q_norm[i], min(512, t))
            x = out_mm(x, o.reshape(m, DIFF_W), o_mem.reshape(m, MEM_W), wout_b, i, tm)
            x = yield x, i, 1
        return x.reshape(bsz, t, d), jnp.stack(states), jnp.stack(shifts), k_out, v_out

    n_pool = cache_k.shape[0]
    cache_kt = jnp.transpose(cache_k, (0, 2, 3, 4, 1)).reshape(n_pool, DIFF_HEADS, DIFF_DV, PAGE_SIZE)
    cache_vt = jnp.transpose(cache_v, (0, 2, 1, 3))

    bp, tp, _ = x_prompt.shape
    bs, ts, _ = x_sample.shape
    assert ts == 1, "the sample group is decoded one token per sequence"
    mem2 = mem_prompt.reshape(bp * MEM_TOKENS, d)
    mk_p, mv_p = [], []
    for i in range(DEPTH):
        mk_i, mv_i = mem_kv(mem2, mem_norm[i], memw_b[i], mem_k_norm[i])
        mk_p.append(mk_i.reshape(bp, MEM_TOKENS, MEM_W))
        mv_p.append(mv_i.reshape(bp, MEM_TOKENS, MEM_W))

    mk_s = cache_mem_k.reshape(DEPTH, bs, MEM_TOKENS, MEM_W)
    mv_s = cache_mem_v.reshape(DEPTH, bs, MEM_TOKENS, MEM_W)
    groups = [run(x_prompt, jnp.zeros((N_A, bp, d), F32), None, mk_p, mv_p, jnp.arange(tp), decode=False),
              run(x_sample, state_shift, state_wkv, mk_s, mv_s, jnp.full((1,), PAST_LEN, jnp.int32), decode=True)]
    requests = [next(gen) for gen in groups]
    results = [None, None]
    while results[0] is None:
        (x_p, layer, half), (x_s, layer_s, half_s) = requests
        assert (layer, half) == (layer_s, half_s)
        outs = ffn(x_p, x_s, ffn_norm[layer, half], ffn_w13, ffn_w2, layer, half)
        for n, gen in enumerate(groups):
            try:
                requests[n] = gen.send(outs[n])
            except StopIteration as done:
                results[n] = done.value
    assert results[1] is not None
    (y_p, wkv_p, shift_p, k_p, v_p), (y_s, wkv_s, shift_s, k_s, v_s) = results

    memshape = lambda zs: jnp.stack(zs).reshape(DEPTH, bp, MEM_TOKENS, MEM_HEADS, MEM_DH)
    return (y_p, y_s, wkv_p, shift_p, wkv_s, shift_s, k_p, v_p, k_s, v_s, memshape(mk_p), memshape(mv_p))
```
